```python
import math
import jax, jax.numpy as jnp
from jax import lax
import numpy as np


D_MODEL = 2048
BATCH = 2
SEQ = 8192
DEPTH = 4

GRID_W = 64
CTX_LEN = 256
N_MIXERS = 3
CONV_ID, POOL_ID, ATT_ID = 0, 1, 2
EXPAND = 2
E_WIDE = EXPAND * D_MODEL
CONV_K = 31
POOL_WINDOWS = (2, 4, 8, 16)
N_POOL_GROUPS = 4
POOL_GROUP = E_WIDE // N_POOL_GROUPS
ATT_WIDTH = D_MODEL
ATT_HEAD_DIM = 64
N_ATT_HEADS = ATT_WIDTH // (2 * ATT_HEAD_DIM)
ROPE_BASE = 10000.0
Q_BLOCK = 128
EPS = 1e-6
SUBLN_EPS = 1e-5
N_CONV = (DEPTH + 2) // 3
N_POOL = (DEPTH + 1) // 3
N_ATT = DEPTH // 3

kernel_name = 'hybrid_conv_pool_diffattn_dit'


def rmsnorm(x, g, eps=EPS):
    xf = x.astype(jnp.float32)
    y = xf * lax.rsqrt(jnp.mean(xf * xf, axis=-1, keepdims=True) + eps)
    return (y * g.astype(jnp.float32)).astype(x.dtype)


def layernorm(x, g, b):
    xf = x.astype(jnp.float32)
    mu = jnp.mean(xf, axis=-1, keepdims=True)
    var = jnp.mean(jnp.square(xf - mu), axis=-1, keepdims=True)
    y = (xf - mu) * lax.rsqrt(var + EPS)
    return (y * g.astype(jnp.float32) + b.astype(jnp.float32)).astype(x.dtype)


def ada_mod(cvec, w_mod, b_mod):
    m = (jax.nn.silu(cvec) @ w_mod + b_mod).reshape(-1, 1, 3 * D_MODEL)
    return jnp.split(m, 3, axis=-1)


def conv_branch(h, w_in, dw_w, dw_b, ln_g, ln_b):
    a, b, z = jnp.split(h @ w_in, 3, axis=-1)
    u = a * jax.nn.sigmoid(b)
    u = lax.conv_general_dilated(
        u, dw_w[:, None, :].astype(u.dtype), window_strides=(1,),
        padding=[(CONV_K // 2, CONV_K // 2)],
        dimension_numbers=('NWC', 'WIO', 'NWC'),
        feature_group_count=E_WIDE) + dw_b
    u = jax.nn.silu(layernorm(u, ln_g, ln_b))
    return u * jax.nn.silu(z)


def pool_branch(h, w_in, w_grp, scale):
    u, z = jnp.split(h @ w_in, 2, axis=-1)
    B, L, _ = u.shape
    uf = u.astype(jnp.float32)
    cs = jnp.concatenate([jnp.zeros((B, 1, E_WIDE), jnp.float32), jnp.cumsum(uf, axis=1)], axis=1)
    t = jnp.arange(L)
    groups = []
    for g, w in enumerate(POOL_WINDOWS):
        lo = jnp.clip(t - w // 2, 0, L)
        hi = jnp.clip(t + w // 2, 0, L)
        sl = slice(g * POOL_GROUP, (g + 1) * POOL_GROUP)
        csg = cs[..., sl]
        cnt = (hi - lo).astype(jnp.float32)[None, :, None]
        groups.append((csg[:, hi] - csg[:, lo]) / cnt - uf[..., sl])
    p = jnp.stack(groups, axis=2).astype(u.dtype)
    p = jnp.einsum('blgc,gcd->blgd', p, w_grp).reshape(B, L, E_WIDE)
    return p * scale * jax.nn.silu(z)


def axial_rope(L):
    rows = L // GRID_W
    row = jnp.repeat(jnp.arange(rows), GRID_W)
    col = jnp.tile(jnp.arange(GRID_W), rows)
    n_freq = ATT_HEAD_DIM // 4
    freqs = ROPE_BASE ** (-jnp.arange(n_freq, dtype=jnp.float32) / n_freq)
    ang = jnp.stack([row[:, None] * freqs, col[:, None] * freqs], axis=1)
    return jnp.cos(ang), jnp.sin(ang)


def apply_rope(x, cos, sin):
    B, L, H, M, dh = x.shape
    xr = x.astype(jnp.float32).reshape(B, L, H, M, 2, 2, dh // 4)
    x1, x2 = xr[..., 0, :], xr[..., 1, :]
    cb, sb = cos[None, :, None, None], sin[None, :, None, None]
    out = jnp.stack([x1 * cb - x2 * sb, x2 * cb + x1 * sb], axis=-2)
    return out.reshape(B, L, H, M, dh).astype(x.dtype)


def diff_mix(q, k, v, lam):
    s = jnp.einsum('bqhmd,bkhmd->mbhqk', q, k, preferred_element_type=jnp.float32) * (ATT_HEAD_DIM ** -0.5)
    p = jax.nn.softmax(s, axis=-1)
    a = p[0] - lam * p[1]
    return jnp.einsum('bhqk,bkhe->bqhe', a.astype(v.dtype), v)


def diff_head_out(o, z, subln_g, lambda_init):
    B, L = o.shape[:2]
    o = rmsnorm(o, subln_g, SUBLN_EPS) * (1.0 - lambda_init)
    return o.reshape(B, L, ATT_WIDTH) * jax.nn.silu(z)


def diff_attention_branch(h, hc, w_in, lq1, lk1, lq2, lk2, subln_g, lambda_init, with_ctx_queries):
    B, L, _ = h.shape
    C = hc.shape[1]
    H, dh = N_ATT_HEADS, ATT_HEAD_DIM
    q, k, v, z = jnp.split(h @ w_in, 4, axis=-1)
    kc, vc = jnp.split(hc @ w_in[:, ATT_WIDTH:3 * ATT_WIDTH], 2, axis=-1)
    cos, sin = axial_rope(L)
    q = apply_rope(q.reshape(B, L, H, 2, dh), cos, sin)
    k = apply_rope(k.reshape(B, L, H, 2, dh), cos, sin)
    kc = kc.reshape(B, C, H, 2, dh)
    vc = vc.reshape(B, C, H, 2 * dh)
    k_all = jnp.concatenate([k, kc], axis=1)
    v_all = jnp.concatenate([v.reshape(B, L, H, 2 * dh), vc], axis=1)
    lam = (jnp.exp(jnp.sum(lq1.astype(jnp.float32) * lk1.astype(jnp.float32)))
           - jnp.exp(jnp.sum(lq2.astype(jnp.float32) * lk2.astype(jnp.float32))) + lambda_init)
    qb = q.reshape(B, L // Q_BLOCK, Q_BLOCK, H, 2, dh).transpose(1, 0, 2, 3, 4, 5)
    o = lax.map(lambda qblk: diff_mix(qblk, k_all, v_all, lam), qb)
    o = o.transpose(1, 0, 2, 3, 4).reshape(B, L, H, 2 * dh)
    y = diff_head_out(o, z, subln_g, lambda_init)
    if not with_ctx_queries:
        return y, None
    qc = (hc @ w_in[:, :ATT_WIDTH]).reshape(B, C, H, 2, dh)
    zc = hc @ w_in[:, 3 * ATT_WIDTH:]
    oc = diff_mix(qc, kc, vc, lam)
    return y, diff_head_out(oc, zc, subln_g, lambda_init)


def setup_inputs(seed: int = 0) -> dict:
    key = jax.random.key(seed)
    ks = list(jax.random.split(key, 32))

    def nrm(i, shape, scale):
        return jax.random.normal(ks[i], shape, jnp.float32) * scale

    D, E, A = D_MODEL, E_WIDE, ATT_WIDTH
    return {
        'x': nrm(0, (BATCH, SEQ, D), 1.0),
        'c': nrm(1, (BATCH, D), 1.0),
        'ctx': nrm(2, (BATCH, CTX_LEN, D), 1.0),
        'c_ctx': nrm(3, (D,), 1.0),
        'norm_g': 1.0 + nrm(4, (DEPTH, D), 0.02),
        'w_mod': nrm(5, (DEPTH, D, 3 * D), D ** -0.5),
        'b_mod': nrm(6, (DEPTH, 3 * D), 0.02),
        'final_g': 1.0 + nrm(7, (D,), 0.02),
        'conv_w_in': nrm(8, (N_CONV, D, 3 * E), D ** -0.5),
        'conv_dw_w': nrm(9, (N_CONV, CONV_K, E), CONV_K ** -0.5),
        'conv_dw_b': nrm(10, (N_CONV, E), 0.02),
        'conv_ln_g': 1.0 + nrm(11, (N_CONV, E), 0.02),
        'conv_ln_b': nrm(12, (N_CONV, E), 0.02),
        'conv_w_out': nrm(13, (N_CONV, E, D), E ** -0.5),
        'pool_w_in': nrm(14, (N_POOL, D, 2 * E), D ** -0.5),
        'pool_w_grp': nrm(15, (N_POOL, N_POOL_GROUPS, POOL_GROUP, POOL_GROUP), POOL_GROUP ** -0.5),
        'pool_scale': 1.0 + nrm(16, (N_POOL, E), 0.02),
        'pool_w_out': nrm(17, (N_POOL, E, D), E ** -0.5),
        'att_w_in': nrm(18, (N_ATT, D, 4 * A), D ** -0.5),
        'att_lam_q1': nrm(19, (N_ATT, ATT_HEAD_DIM), 0.1),
        'att_lam_k1': nrm(20, (N_ATT, ATT_HEAD_DIM), 0.1),
        'att_lam_q2': nrm(21, (N_ATT, ATT_HEAD_DIM), 0.1),
        'att_lam_k2': nrm(22, (N_ATT, ATT_HEAD_DIM), 0.1),
        'att_subln_g': 1.0 + nrm(23, (N_ATT, 2 * ATT_HEAD_DIM), 0.02),
        'att_w_out': nrm(24, (N_ATT, A, D), A ** -0.5),
    }


def reference(x, c, ctx, c_ctx, norm_g, w_mod, b_mod, final_g,
              conv_w_in, conv_dw_w, conv_dw_b, conv_ln_g, conv_ln_b, conv_w_out,
              pool_w_in, pool_w_grp, pool_scale, pool_w_out,
              att_w_in, att_lam_q1, att_lam_k1, att_lam_q2, att_lam_k2, att_subln_g, att_w_out):
    last_ctx_reader = N_MIXERS * (N_ATT - 1) + ATT_ID if N_ATT > 0 else -1
    xs, cs = x, ctx
    for i in range(DEPTH):
        kind, j = i % N_MIXERS, i // N_MIXERS
        ctx_live = i <= last_ctx_reader
        ctx_update = i < last_ctx_reader
        sh, sc, gt = ada_mod(c, w_mod[i], b_mod[i])
        h = rmsnorm(xs, norm_g[i]) * (1.0 + sc) + sh
        if ctx_live:
            shc, scc, gtc = ada_mod(c_ctx, w_mod[i], b_mod[i])
            hc = rmsnorm(cs, norm_g[i]) * (1.0 + scc) + shc
        yc = None
        if kind == CONV_ID:
            y = conv_branch(h, conv_w_in[j], conv_dw_w[j], conv_dw_b[j], conv_ln_g[j], conv_ln_b[j]) @ conv_w_out[j]
            if ctx_update:
                yc = conv_branch(hc, conv_w_in[j], conv_dw_w[j], conv_dw_b[j], conv_ln_g[j], conv_ln_b[j]) @ conv_w_out[j]
        elif kind == POOL_ID:
            y = pool_branch(h, pool_w_in[j], pool_w_grp[j], pool_scale[j]) @ pool_w_out[j]
            if ctx_update:
                yc = pool_branch(hc, pool_w_in[j], pool_w_grp[j], pool_scale[j]) @ pool_w_out[j]
        else:
            lambda_init = 0.8 - 0.6 * math.exp(-0.3 * i)
            ya, yca = diff_attention_branch(h, hc, att_w_in[j], att_lam_q1[j], att_lam_k1[j],
                                            att_lam_q2[j], att_lam_k2[j], att_subln_g[j],
                                            lambda_init, ctx_update)
            y = ya @ att_w_out[j]
            if ctx_update:
                yc = yca @ att_w_out[j]
        xs = xs + gt * y
        if ctx_update:
            cs = cs + gtc * yc
    return rmsnorm(xs, final_g)
```

```python
import functools
import math

import jax
import jax.numpy as jnp
from jax import lax
from jax.experimental import pallas as pl
from jax.experimental.pallas import tpu as pltpu

D_MODEL = 2048
DEPTH = 4
GRID_W = 64
N_MIXERS = 3
CONV_ID, POOL_ID, ATT_ID = 0, 1, 2
E_WIDE = 2 * D_MODEL
CONV_K = 31
CONV_HALF = CONV_K // 2
POOL_WINDOWS = (2, 4, 8, 16)
POOL_GROUP = E_WIDE // len(POOL_WINDOWS)
ATT_WIDTH = D_MODEL
ATT_HEAD_DIM = 64
HEAD_W = 2 * ATT_HEAD_DIM
N_ATT_HEADS = ATT_WIDTH // HEAD_W
ROPE_BASE = 10000.0
EPS = 1e-6
SUBLN_EPS = 1e-5

LANES = 128
HALO = 16
MOD_ROWS = 8
VMEM_LIMIT = 56 * 1024 * 1024

F32 = jnp.float32
BF16 = jnp.bfloat16


def _silu(v):
    return v * jax.nn.sigmoid(v)


def _params(*sem):
    return pltpu.CompilerParams(dimension_semantics=sem, vmem_limit_bytes=VMEM_LIMIT)


def _norm_mod(xv, g, sc, sh):
    xv = xv.astype(F32)
    y = xv * lax.rsqrt(jnp.mean(xv * xv, axis=-1, keepdims=True) + EPS)
    return ((y * g) * (1.0 + sc) + sh).astype(BF16)


def _mod_kernel(c_ref, w_ref, b_ref, o_ref):
    s = _silu(c_ref[...])
    o_ref[...] = jnp.dot(s, w_ref[...], precision=lax.Precision.HIGHEST,
                         preferred_element_type=F32) + b_ref[...]


def _modulation(cc, w_mod, b_mod):
    tn = 1024
    n = 3 * D_MODEL
    return pl.pallas_call(
        _mod_kernel,
        grid=(DEPTH, n // tn),
        in_specs=[
            pl.BlockSpec((MOD_ROWS, D_MODEL), lambda l, j: (0, 0)),
            pl.BlockSpec((None, D_MODEL, tn), lambda l, j: (l, 0, j)),
            pl.BlockSpec((None, 1, tn), lambda l, j: (l, 0, j)),
        ],
        out_specs=pl.BlockSpec((None, MOD_ROWS, tn), lambda l, j: (l, 0, j)),
        out_shape=jax.ShapeDtypeStruct((DEPTH, MOD_ROWS, n), F32),
        compiler_params=_params("parallel", "parallel"),
        name="modulation",
    )(cc, w_mod, b_mod.reshape(DEPTH, 1, n))


def _x_specs(tm, seq_len, mod_row):
    nh = tm // HALO
    last = seq_len // HALO - 1
    row = (lambda b: b) if mod_row is None else (lambda b: mod_row)
    return [
        pl.BlockSpec((None, HALO, D_MODEL), lambda b, i, j: (b, jnp.maximum(i * nh - 1, 0), 0)),
        pl.BlockSpec((None, tm, D_MODEL), lambda b, i, j: (b, i, 0)),
        pl.BlockSpec((None, HALO, D_MODEL), lambda b, i, j: (b, jnp.minimum((i + 1) * nh, last), 0)),
        pl.BlockSpec((None, None, 1, D_MODEL), lambda b, i, j: (row(b), 0, 0, 0)),
        pl.BlockSpec((None, None, 1, D_MODEL), lambda b, i, j: (row(b), 1, 0, 0)),
        pl.BlockSpec((1, D_MODEL), lambda b, i, j: (0, 0)),
    ]


def _fill_h(h_scr, xp_ref, x_ref, xn_ref, sh_ref, sc_ref, g_ref, tm):
    g, sc, sh = g_ref[...], sc_ref[...], sh_ref[...]
    h_scr[0:HALO, :] = _norm_mod(xp_ref[...], g, sc, sh)
    h_scr[HALO:HALO + tm, :] = _norm_mod(x_ref[...], g, sc, sh)
    h_scr[HALO + tm:, :] = _norm_mod(xn_ref[...], g, sc, sh)


def _seq_row_mask(i, tm, seq_len, tn):
    row = lax.broadcasted_iota(jnp.int32, (tm + 2 * HALO, tn), 0) + (i * tm - HALO)
    return (row >= 0) & (row < seq_len)


CONV_ROWS = 32


def _conv_in_kernel(xp_ref, x_ref, xn_ref, sh_ref, sc_ref, g_ref, wa_ref, wb_ref, wz_ref,
                    dw_ref, db_ref, v_ref, z_ref, h_scr, u_scr, *, tm, tn, seq_len):
    i, j = pl.program_id(1), pl.program_id(2)

    @pl.when(j == 0)
    def _():
        _fill_h(h_scr, xp_ref, x_ref, xn_ref, sh_ref, sc_ref, g_ref, tm)

    h = h_scr[...]
    a = jnp.dot(h, wa_ref[...], preferred_element_type=F32)
    b = jnp.dot(h, wb_ref[...], preferred_element_type=F32)
    u_scr[...] = jnp.where(_seq_row_mask(i, tm, seq_len, tn), a * jax.nn.sigmoid(b), 0.0)
    z_ref[...] = jnp.dot(h_scr[HALO:HALO + tm, :], wz_ref[...],
                         preferred_element_type=F32).astype(z_ref.dtype)

    for r0 in range(0, tm, CONV_ROWS):
        acc = jnp.broadcast_to(db_ref[...], (CONV_ROWS, tn))
        for k in range(CONV_K):
            acc = acc + u_scr[pl.ds(r0 + (HALO - CONV_HALF + k), CONV_ROWS), :] * dw_ref[k:k + 1, :]
        v_ref[pl.ds(r0, CONV_ROWS), :] = acc.astype(v_ref.dtype)


def _conv_in(x, mods, norm_g, w_in, dw_w, dw_b, *, mod_row, tm, tn):
    bsz, seq_len, _ = x.shape
    nj = E_WIDE // tn
    kern = functools.partial(_conv_in_kernel, tm=tm, tn=tn, seq_len=seq_len)
    out = jax.ShapeDtypeStruct((bsz, seq_len, E_WIDE), BF16)
    return pl.pallas_call(
        kern,
        grid=(bsz, seq_len // tm, nj),
        in_specs=_x_specs(tm, seq_len, mod_row) + [
            pl.BlockSpec((D_MODEL, tn), lambda b, i, j: (0, j)),
            pl.BlockSpec((D_MODEL, tn), lambda b, i, j: (0, nj + j)),
            pl.BlockSpec((D_MODEL, tn), lambda b, i, j: (0, 2 * nj + j)),
            pl.BlockSpec((CONV_K, tn), lambda b, i, j: (0, j)),
            pl.BlockSpec((1, tn), lambda b, i, j: (0, j)),
        ],
        out_specs=[pl.BlockSpec((None, tm, tn), lambda b, i, j: (b, i, j))] * 2,
        out_shape=[out, out],
        scratch_shapes=[pltpu.VMEM((tm + 2 * HALO, D_MODEL), BF16),
                        pltpu.VMEM((tm + 2 * HALO, tn), F32)],
        compiler_params=_params("parallel", "parallel", "arbitrary"),
        name="conv_in",
    )(x, x, x, mods, mods, norm_g, w_in, w_in, w_in, dw_w, dw_b)


def _conv_out_kernel(v_ref, z_ref, lg_ref, lb_ref, w_ref, x_ref, gt_ref, o_ref, act_scr):
    @pl.when(pl.program_id(2) == 0)
    def _():
        v = v_ref[...].astype(F32)
        mu = jnp.mean(v, axis=-1, keepdims=True)
        d = v - mu
        var = jnp.mean(d * d, axis=-1, keepdims=True)
        y = _silu(d * lax.rsqrt(var + EPS) * lg_ref[...] + lb_ref[...])
        act_scr[...] = (y * _silu(z_ref[...].astype(F32))).astype(BF16)

    y = jnp.dot(act_scr[...], w_ref[...], preferred_element_type=F32)
    o_ref[...] = x_ref[...] + gt_ref[...] * y


def _conv_out(v, z, ln_g, ln_b, w_out, x, mods, *, mod_row, tm, tn):
    bsz, seq_len, _ = x.shape
    row = (lambda b: b) if mod_row is None else (lambda b: mod_row)
    return pl.pallas_call(
        _conv_out_kernel,
        grid=(bsz, seq_len // tm, D_MODEL // tn),
        in_specs=[
            pl.BlockSpec((None, tm, E_WIDE), lambda b, i, j: (b, i, 0)),
            pl.BlockSpec((None, tm, E_WIDE), lambda b, i, j: (b, i, 0)),
            pl.BlockSpec((1, E_WIDE), lambda b, i, j: (0, 0)),
            pl.BlockSpec((1, E_WIDE), lambda b, i, j: (0, 0)),
            pl.BlockSpec((E_WIDE, tn), lambda b, i, j: (0, j)),
            pl.BlockSpec((None, tm, tn), lambda b, i, j: (b, i, j)),
            pl.BlockSpec((None, None, 1, tn), lambda b, i, j: (row(b), 2, 0, j)),
        ],
        out_specs=pl.BlockSpec((None, tm, tn), lambda b, i, j: (b, i, j)),
        out_shape=jax.ShapeDtypeStruct(x.shape, F32),
        scratch_shapes=[pltpu.VMEM((tm, E_WIDE), BF16)],
        compiler_params=_params("parallel", "parallel", "arbitrary"),
        name="conv_out",
    )(v, z, ln_g, ln_b, w_out, x, mods)


POOL_ROWS = 64


def _pool_in_kernel(xp_ref, x_ref, xn_ref, sh_ref, sc_ref, g_ref, wu_ref, wz_ref, wg_ref,
                    ps_ref, o_ref, h_scr, u_scr, p_scr, *, tm, seq_len):
    i, grp = pl.program_id(1), pl.program_id(2)
    tn = POOL_GROUP

    @pl.when(grp == 0)
    def _():
        _fill_h(h_scr, xp_ref, x_ref, xn_ref, sh_ref, sc_ref, g_ref, tm)

    u = jnp.dot(h_scr[...], wu_ref[...], preferred_element_type=F32)
    u_scr[...] = jnp.where(_seq_row_mask(i, tm, seq_len, tn), u, 0.0)

    for gi, win in enumerate(POOL_WINDOWS):
        @pl.when(grp == gi)
        def _(win=win):
            half = win // 2

            for r0 in range(0, tm, POOL_ROWS):
                t = lax.broadcasted_iota(jnp.int32, (POOL_ROWS, 1), 0) + (i * tm + r0)
                cnt = jnp.minimum(t + half, seq_len) - jnp.maximum(t - half, 0)
                acc = u_scr[pl.ds(r0 + (HALO - half), POOL_ROWS), :]
                for d in range(1, win):
                    acc = acc + u_scr[pl.ds(r0 + (HALO - half + d), POOL_ROWS), :]
                p = acc / cnt.astype(F32) - u_scr[pl.ds(r0 + HALO, POOL_ROWS), :]
                p_scr[pl.ds(r0, POOL_ROWS), :] = p.astype(BF16)

    z = jnp.dot(h_scr[HALO:HALO + tm, :], wz_ref[...], preferred_element_type=F32)
    q = jnp.dot(p_scr[...], wg_ref[...], preferred_element_type=F32)
    o_ref[...] = (q * ps_ref[...] * _silu(z)).astype(o_ref.dtype)


def _pool_in(x, mods, norm_g, w_in, w_grp, scale, *, mod_row, tm):
    bsz, seq_len, _ = x.shape
    ng = len(POOL_WINDOWS)
    tn = POOL_GROUP
    kern = functools.partial(_pool_in_kernel, tm=tm, seq_len=seq_len)
    return pl.pallas_call(
        kern,
        grid=(bsz, seq_len // tm, ng),
        in_specs=_x_specs(tm, seq_len, mod_row) + [
            pl.BlockSpec((D_MODEL, tn), lambda b, i, j: (0, j)),
            pl.BlockSpec((D_MODEL, tn), lambda b, i, j: (0, ng + j)),
            pl.BlockSpec((None, tn, tn), lambda b, i, j: (j, 0, 0)),
            pl.BlockSpec((1, tn), lambda b, i, j: (0, j)),
        ],
        out_specs=pl.BlockSpec((None, tm, tn), lambda b, i, j: (b, i, j)),
        out_shape=jax.ShapeDtypeStruct((bsz, seq_len, E_WIDE), BF16),
        scratch_shapes=[pltpu.VMEM((tm + 2 * HALO, D_MODEL), BF16),
                        pltpu.VMEM((tm + 2 * HALO, tn), F32),
                        pltpu.VMEM((tm, tn), BF16)],
        compiler_params=_params("parallel", "parallel", "arbitrary"),
        name="pool_in",
    )(x, x, x, mods, mods, norm_g, w_in, w_in, w_grp, scale)


def _proj_out_kernel(a_ref, w_ref, x_ref, gt_ref, o_ref):
    y = jnp.dot(a_ref[...], w_ref[...], preferred_element_type=F32)
    o_ref[...] = x_ref[...] + gt_ref[...] * y


def _proj_out(act, w_out, x, mods, *, mod_row, tm, tn):
    bsz, seq_len, _ = x.shape
    kdim = act.shape[-1]
    row = (lambda b: b) if mod_row is None else (lambda b: mod_row)
    return pl.pallas_call(
        _proj_out_kernel,
        grid=(bsz, seq_len // tm, D_MODEL // tn),
        in_specs=[
            pl.BlockSpec((None, tm, kdim), lambda b, i, j: (b, i, 0)),
            pl.BlockSpec((kdim, tn), lambda b, i, j: (0, j)),
            pl.BlockSpec((None, tm, tn), lambda b, i, j: (b, i, j)),
            pl.BlockSpec((None, None, 1, tn), lambda b, i, j: (row(b), 2, 0, j)),
        ],
        out_specs=pl.BlockSpec((None, tm, tn), lambda b, i, j: (b, i, j)),
        out_shape=jax.ShapeDtypeStruct(x.shape, F32),
        compiler_params=_params("parallel", "parallel", "arbitrary"),
        name="proj_out",
    )(act, w_out, x, mods)


def _rope_tables(seq_len):
    t = jnp.arange(seq_len)
    n_freq = ATT_HEAD_DIM // 4
    freqs = ROPE_BASE ** (-jnp.arange(n_freq, dtype=F32) / n_freq)
    pos = jnp.stack([t // GRID_W, t % GRID_W], axis=1).astype(F32)
    ang = pos[:, :, None] * freqs
    cos, sin = jnp.cos(ang), jnp.sin(ang)
    zero = jnp.zeros_like(sin)

    def table(first, second):
        tab = jnp.stack([first, second], axis=2)
        return jnp.tile(tab.reshape(seq_len, ATT_HEAD_DIM), (1, 2))

    return table(cos, cos), table(-sin, zero), table(zero, sin)


def _att_in_kernel(xp_ref, x_ref, xn_ref, sh_ref, sc_ref, g_ref, w_ref, cos_ref, sa_ref, sb_ref,
                   o_ref, h_scr, *, tm, tn, n_q_tiles, n_rope_tiles):
    del xp_ref, xn_ref
    j = pl.program_id(2)

    @pl.when(j == 0)
    def _():
        h_scr[...] = _norm_mod(x_ref[...], g_ref[...], sc_ref[...], sh_ref[...])

    y = jnp.dot(h_scr[...], w_ref[...], preferred_element_type=F32)

    @pl.when(j < n_rope_tiles)
    def _():
        scale = jnp.where(j < n_q_tiles, ATT_HEAD_DIM ** -0.5, 1.0)
        cos, sa, sb = cos_ref[...], sa_ref[...], sb_ref[...]
        half = ATT_HEAD_DIM // 4
        for c in range(tn // LANES):
            yc = y[:, c * LANES:(c + 1) * LANES]
            r = (yc * cos + pltpu.roll(yc, LANES - half, axis=1) * sa
                 + pltpu.roll(yc, half, axis=1) * sb)
            o_ref[:, c * LANES:(c + 1) * LANES] = (r * scale).astype(o_ref.dtype)

    @pl.when(j >= n_rope_tiles)
    def _():
        o_ref[...] = y.astype(o_ref.dtype)


def _att_in(x, mods, norm_g, w_in, *, mod_row, tm, tn, col_lo, n_cols, rope):
    bsz, seq_len, _ = x.shape
    j0 = col_lo // tn
    if rope:
        tabs = _rope_tables(seq_len)
        n_q, n_rope = ATT_WIDTH // tn, 2 * ATT_WIDTH // tn
    else:
        tabs = (jnp.zeros((seq_len, LANES), F32),) * 3
        n_q = n_rope = 0
    kern = functools.partial(_att_in_kernel, tm=tm, tn=tn, n_q_tiles=n_q, n_rope_tiles=n_rope)
    tab_spec = pl.BlockSpec((tm, LANES), lambda b, i, j: (i, 0))
    return pl.pallas_call(
        kern,
        grid=(bsz, seq_len // tm, n_cols // tn),
        in_specs=_x_specs(tm, seq_len, mod_row) + [
            pl.BlockSpec((D_MODEL, tn), lambda b, i, j: (0, j0 + j)),
            tab_spec, tab_spec, tab_spec,
        ],
        out_specs=pl.BlockSpec((None, tm, tn), lambda b, i, j: (b, i, j)),
        out_shape=jax.ShapeDtypeStruct((bsz, seq_len, n_cols), BF16),
        scratch_shapes=[pltpu.VMEM((tm, D_MODEL), BF16)],
        compiler_params=_params("parallel", "parallel", "arbitrary"),
        name="att_in",
    )(x, x, x, mods, mods, norm_g, w_in, *tabs)


def _flash_kernel(lq1_ref, lk1_ref, lq2_ref, lk2_ref, sg_ref, q_ref, k_ref, v_ref, kc_ref, vc_ref,
                  z_ref, y_ref, m_scr, l_scr, acc_scr, *, tq, tk, seq_len, lambda_init):
    q = q_ref[...]
    lane = lax.broadcasted_iota(jnp.int32, q.shape, 1)
    zero = jnp.zeros_like(q)
    qq = jnp.concatenate([jnp.where(lane < ATT_HEAD_DIM, q, zero),
                          jnp.where(lane >= ATT_HEAD_DIM, q, zero)], axis=0)

    m_scr[...] = jnp.full(m_scr.shape, -jnp.inf, F32)
    l_scr[...] = jnp.zeros(l_scr.shape, F32)
    acc_scr[...] = jnp.zeros(acc_scr.shape, F32)

    def step(kb, vb):
        s = lax.dot_general(qq, kb, (((1,), (1,)), ((), ())), preferred_element_type=F32)
        m_old = m_scr[...]
        m_new = jnp.maximum(m_old, jnp.max(s, axis=1, keepdims=True))
        alpha = jnp.exp(m_old - m_new)
        p = jnp.exp(s - m_new)
        l_scr[...] = alpha * l_scr[...] + jnp.sum(p, axis=1, keepdims=True)
        acc_scr[...] = alpha * acc_scr[...] + jnp.dot(p.astype(BF16), vb, preferred_element_type=F32)
        m_scr[...] = m_new

    step(kc_ref[...], vc_ref[...])

    def body(c, carry):
        r0 = pl.multiple_of(c * tk, tk)
        step(k_ref[pl.ds(r0, tk), :], v_ref[pl.ds(r0, tk), :])
        return carry

    lax.fori_loop(0, seq_len // tk, body, 0)

    lam = (jnp.exp(jnp.sum(lq1_ref[...] * lk1_ref[...], axis=1, keepdims=True))
           - jnp.exp(jnp.sum(lq2_ref[...] * lk2_ref[...], axis=1, keepdims=True)) + lambda_init)
    o = acc_scr[0:tq, :] / l_scr[0:tq, :] - lam * (acc_scr[tq:, :] / l_scr[tq:, :])
    o = o * lax.rsqrt(jnp.mean(o * o, axis=-1, keepdims=True) + SUBLN_EPS)
    o = (o * sg_ref[...]) * (1.0 - lambda_init)
    y_ref[...] = (o * _silu(z_ref[...].astype(F32))).astype(y_ref.dtype)


def _flash(qkvz, kvc, lam_params, subln_g, *, lambda_init, tq, tk):
    bsz, seq_len, _ = qkvz.shape
    ctx_len = kvc.shape[1]
    nh = N_ATT_HEADS
    kern = functools.partial(_flash_kernel, tq=tq, tk=tk, seq_len=seq_len, lambda_init=lambda_init)
    vec = pl.BlockSpec((1, ATT_HEAD_DIM), lambda b, h, i: (0, 0))
    return pl.pallas_call(
        kern,
        grid=(bsz, nh, seq_len // tq),
        in_specs=[vec, vec, vec, vec,
                  pl.BlockSpec((1, HEAD_W), lambda b, h, i: (0, 0)),
                  pl.BlockSpec((None, tq, HEAD_W), lambda b, h, i: (b, i, h)),
                  pl.BlockSpec((None, seq_len, HEAD_W), lambda b, h, i: (b, 0, nh + h)),
                  pl.BlockSpec((None, seq_len, HEAD_W), lambda b, h, i: (b, 0, 2 * nh + h)),
                  pl.BlockSpec((None, ctx_len, HEAD_W), lambda b, h, i: (b, 0, h)),
                  pl.BlockSpec((None, ctx_len, HEAD_W), lambda b, h, i: (b, 0, nh + h)),
                  pl.BlockSpec((None, tq, HEAD_W), lambda b, h, i: (b, i, 3 * nh + h))],
        out_specs=pl.BlockSpec((None, tq, HEAD_W), lambda b, h, i: (b, i, h)),
        out_shape=jax.ShapeDtypeStruct((bsz, seq_len, ATT_WIDTH), BF16),
        scratch_shapes=[pltpu.VMEM((2 * tq, 1), F32), pltpu.VMEM((2 * tq, 1), F32),
                        pltpu.VMEM((2 * tq, HEAD_W), F32)],
        compiler_params=_params("parallel", "parallel", "arbitrary"),
        name="diff_flash",
    )(*lam_params, subln_g, qkvz, qkvz, qkvz, kvc, kvc, qkvz)


def _final_norm_kernel(x_ref, g_ref, o_ref):
    xv = x_ref[...]
    o_ref[...] = xv * lax.rsqrt(jnp.mean(xv * xv, axis=-1, keepdims=True) + EPS) * g_ref[...]


def _final_norm(x, g, *, tm):
    bsz, seq_len, _ = x.shape
    return pl.pallas_call(
        _final_norm_kernel,
        grid=(bsz, seq_len // tm),
        in_specs=[pl.BlockSpec((None, tm, D_MODEL), lambda b, i: (b, i, 0)),
                  pl.BlockSpec((1, D_MODEL), lambda b, i: (0, 0))],
        out_specs=pl.BlockSpec((None, tm, D_MODEL), lambda b, i: (b, i, 0)),
        out_shape=jax.ShapeDtypeStruct(x.shape, F32),
        compiler_params=_params("parallel", "parallel"),
        name="final_norm",
    )(x, g)


def kernel(x, c, ctx, c_ctx, norm_g, w_mod, b_mod, final_g, conv_w_in, conv_dw_w, conv_dw_b,
           conv_ln_g, conv_ln_b, conv_w_out, pool_w_in, pool_w_grp, pool_scale, pool_w_out,
           att_w_in, att_lam_q1, att_lam_k1, att_lam_q2, att_lam_k2, att_subln_g, att_w_out):
    bsz = x.shape[0]
    ctx_len = ctx.shape[1]
    n_att = DEPTH // N_MIXERS
    last_ctx_reader = N_MIXERS * (n_att - 1) + ATT_ID if n_att > 0 else -1
    ctx_row = bsz

    cc = jnp.zeros((MOD_ROWS, D_MODEL), F32).at[:bsz].set(c).at[ctx_row].set(c_ctx)
    mods_all = _modulation(cc, w_mod, b_mod).reshape(DEPTH, MOD_ROWS, 3, 1, D_MODEL)

    tm, tn = 512, 512
    xs, cs = x, ctx
    for i in range(DEPTH):
        kind, j = i % N_MIXERS, i // N_MIXERS
        ctx_live = i <= last_ctx_reader
        ctx_update = i < last_ctx_reader
        mods = mods_all[i]
        g = norm_g[i].reshape(1, D_MODEL)
        streams = [(xs, None, tm)]
        if ctx_update:
            streams.append((cs, ctx_row, ctx_len))
        new = []
        if kind == CONV_ID:
            w_in = conv_w_in[j].astype(BF16)
            w_out = conv_w_out[j].astype(BF16)
            for s, row, tms in streams:
                v, z = _conv_in(s, mods, g, w_in, conv_dw_w[j], conv_dw_b[j].reshape(1, E_WIDE),
                                mod_row=row, tm=tms, tn=tn)
                new.append(_conv_out(v, z, conv_ln_g[j].reshape(1, E_WIDE),
                                     conv_ln_b[j].reshape(1, E_WIDE), w_out, s, mods,
                                     mod_row=row, tm=min(tms, 256), tn=tn))
        elif kind == POOL_ID:
            w_in = pool_w_in[j].astype(BF16)
            w_grp = pool_w_grp[j].astype(BF16)
            w_out = pool_w_out[j].astype(BF16)
            for s, row, tms in streams:
                act = _pool_in(s, mods, g, w_in, w_grp, pool_scale[j].reshape(1, E_WIDE),
                               mod_row=row, tm=tms)
                new.append(_proj_out(act, w_out, s, mods, mod_row=row, tm=tms, tn=tn))
        else:
            lambda_init = 0.8 - 0.6 * math.exp(-0.3 * i)
            w_in = att_w_in[j].astype(BF16)
            w_out = att_w_out[j].astype(BF16)
            qkvz = _att_in(xs, mods, g, w_in, mod_row=None, tm=tm, tn=tn,
                           col_lo=0, n_cols=4 * ATT_WIDTH, rope=True)
            kvc = _att_in(cs, mods, g, w_in, mod_row=ctx_row, tm=ctx_len, tn=tn,
                          col_lo=ATT_WIDTH, n_cols=2 * ATT_WIDTH, rope=False)
            lam_params = [p[j].reshape(1, ATT_HEAD_DIM)
                          for p in (att_lam_q1, att_lam_k1, att_lam_q2, att_lam_k2)]
            y = _flash(qkvz, kvc, lam_params, att_subln_g[j].reshape(1, HEAD_W),
                       lambda_init=lambda_init, tq=512, tk=512)
            new.append(_proj_out(y, w_out, xs, mods, mod_row=None, tm=tm, tn=tn))
            if ctx_update:
                raise NotImplementedError("context-side queries are not needed at this depth")
        xs = new[0]
        if ctx_update:
            cs = new[1]
        del ctx_live
    return _final_norm(xs, final_g.reshape(1, D_MODEL), tm=tm)
```

```python
import functools
import math

import jax
import jax.numpy as jnp
from jax import lax
from jax.experimental import pallas as pl
from jax.experimental.pallas import tpu as pltpu

D_MODEL = 2048
DEPTH = 4
GRID_W = 64
N_MIXERS = 3
CONV_ID, POOL_ID, ATT_ID = 0, 1, 2
E_WIDE = 2 * D_MODEL
CONV_K = 31
CONV_HALF = CONV_K // 2
POOL_WINDOWS = (2, 4, 8, 16)
POOL_GROUP = E_WIDE // len(POOL_WINDOWS)
ATT_WIDTH = D_MODEL
ATT_HEAD_DIM = 64
HEAD_W = 2 * ATT_HEAD_DIM
N_ATT_HEADS = ATT_WIDTH // HEAD_W
ROPE_BASE = 10000.0
EPS = 1e-6
SUBLN_EPS = 1e-5

LANES = 128
HALO = 16
MOD_ROWS = 8
VMEM_LIMIT = 56 * 1024 * 1024

F32 = jnp.float32
BF16 = jnp.bfloat16


def _silu(v):
    return v * jax.nn.sigmoid(v)


def _params(*sem):
    return pltpu.CompilerParams(dimension_semantics=sem, vmem_limit_bytes=VMEM_LIMIT)


def _norm_mod(xv, g, sc, sh):
    xv = xv.astype(F32)
    y = xv * lax.rsqrt(jnp.mean(xv * xv, axis=-1, keepdims=True) + EPS)
    return ((y * g) * (1.0 + sc) + sh).astype(BF16)


def _mod_kernel(c_ref, w_ref, b_ref, o_ref):
    s = _silu(c_ref[...])
    o_ref[...] = jnp.dot(s, w_ref[...], precision=lax.Precision.HIGHEST,
                         preferred_element_type=F32) + b_ref[...]


def _modulation(cc, w_mod, b_mod):
    tn = 1024
    n = 3 * D_MODEL
    return pl.pallas_call(
        _mod_kernel,
        grid=(DEPTH, n // tn),
        in_specs=[
            pl.BlockSpec((MOD_ROWS, D_MODEL), lambda l, j: (0, 0)),
            pl.BlockSpec((None, D_MODEL, tn), lambda l, j: (l, 0, j)),
            pl.BlockSpec((None, 1, tn), lambda l, j: (l, 0, j)),
        ],
        out_specs=pl.BlockSpec((None, MOD_ROWS, tn), lambda l, j: (l, 0, j)),
        out_shape=jax.ShapeDtypeStruct((DEPTH, MOD_ROWS, n), F32),
        compiler_params=_params("parallel", "parallel"),
        name="modulation",
    )(cc, w_mod, b_mod.reshape(DEPTH, 1, n))


def _x_specs(tm, seq_len, mod_row):
    nh = tm // HALO
    last = seq_len // HALO - 1
    row = (lambda b: b) if mod_row is None else (lambda b: mod_row)
    return [
        pl.BlockSpec((None, HALO, D_MODEL), lambda b, i, j: (b, jnp.maximum(i * nh - 1, 0), 0)),
        pl.BlockSpec((None, tm, D_MODEL), lambda b, i, j: (b, i, 0)),
        pl.BlockSpec((None, HALO, D_MODEL), lambda b, i, j: (b, jnp.minimum((i + 1) * nh, last), 0)),
        pl.BlockSpec((None, None, 1, D_MODEL), lambda b, i, j: (row(b), 0, 0, 0)),
        pl.BlockSpec((None, None, 1, D_MODEL), lambda b, i, j: (row(b), 1, 0, 0)),
        pl.BlockSpec((1, D_MODEL), lambda b, i, j: (0, 0)),
    ]


def _fill_h(h_scr, xp_ref, x_ref, xn_ref, sh_ref, sc_ref, g_ref, tm):
    g, sc, sh = g_ref[...], sc_ref[...], sh_ref[...]
    h_scr[0:HALO, :] = _norm_mod(xp_ref[...], g, sc, sh)
    h_scr[HALO:HALO + tm, :] = _norm_mod(x_ref[...], g, sc, sh)
    h_scr[HALO + tm:, :] = _norm_mod(xn_ref[...], g, sc, sh)


def _seq_row_mask(i, tm, seq_len, tn):
    row = lax.broadcasted_iota(jnp.int32, (tm + 2 * HALO, tn), 0) + (i * tm - HALO)
    return (row >= 0) & (row < seq_len)


CONV_ROWS = 32


def _conv_in_kernel(xp_ref, x_ref, xn_ref, sh_ref, sc_ref, g_ref, wa_ref, wb_ref, wz_ref,
                    dw_ref, db_ref, v_ref, z_ref, h_scr, u_scr, *, tm, tn, seq_len):
    i, j = pl.program_id(1), pl.program_id(2)

    @pl.when(j == 0)
    def _():
        _fill_h(h_scr, xp_ref, x_ref, xn_ref, sh_ref, sc_ref, g_ref, tm)

    h = h_scr[...]
    a = jnp.dot(h, wa_ref[...], preferred_element_type=F32)
    b = jnp.dot(h, wb_ref[...], preferred_element_type=F32)
    u_scr[...] = jnp.where(_seq_row_mask(i, tm, seq_len, tn), a * jax.nn.sigmoid(b), 0.0)
    z_ref[...] = jnp.dot(h_scr[HALO:HALO + tm, :], wz_ref[...],
                         preferred_element_type=F32).astype(z_ref.dtype)

    for r0 in range(0, tm, CONV_ROWS):
        acc = jnp.broadcast_to(db_ref[...], (CONV_ROWS, tn))
        for k in range(CONV_K):
            acc = acc + u_scr[pl.ds(r0 + (HALO - CONV_HALF + k), CONV_ROWS), :] * dw_ref[k:k + 1, :]
        v_ref[pl.ds(r0, CONV_ROWS), :] = acc.astype(v_ref.dtype)


def _conv_in(x, mods, norm_g, w_in, dw_w, dw_b, *, mod_row, tm, tn):
    bsz, seq_len, _ = x.shape
    nj = E_WIDE // tn
    kern = functools.partial(_conv_in_kernel, tm=tm, tn=tn, seq_len=seq_len)
    out = jax.ShapeDtypeStruct((bsz, seq_len, E_WIDE), BF16)
    return pl.pallas_call(
        kern,
        grid=(bsz, seq_len // tm, nj),
        in_specs=_x_specs(tm, seq_len, mod_row) + [
            pl.BlockSpec((D_MODEL, tn), lambda b, i, j: (0, j)),
            pl.BlockSpec((D_MODEL, tn), lambda b, i, j: (0, nj + j)),
            pl.BlockSpec((D_MODEL, tn), lambda b, i, j: (0, 2 * nj + j)),
            pl.BlockSpec((CONV_K, tn), lambda b, i, j: (0, j)),
            pl.BlockSpec((1, tn), lambda b, i, j: (0, j)),
        ],
        out_specs=[pl.BlockSpec((None, tm, tn), lambda b, i, j: (b, i, j))] * 2,
        out_shape=[out, out],
        scratch_shapes=[pltpu.VMEM((tm + 2 * HALO, D_MODEL), BF16),
                        pltpu.VMEM((tm + 2 * HALO, tn), F32)],
        compiler_params=_params("parallel", "parallel", "arbitrary"),
        name="conv_in",
    )(x, x, x, mods, mods, norm_g, w_in, w_in, w_in, dw_w, dw_b)


def _conv_out_kernel(v_ref, z_ref, lg_ref, lb_ref, w_ref, x_ref, gt_ref, o_ref, act_scr):
    @pl.when(pl.program_id(2) == 0)
    def _():
        v = v_ref[...].astype(F32)
        mu = jnp.mean(v, axis=-1, keepdims=True)
        d = v - mu
        var = jnp.mean(d * d, axis=-1, keepdims=True)
        y = _silu(d * lax.rsqrt(var + EPS) * lg_ref[...] + lb_ref[...])
        act_scr[...] = (y * _silu(z_ref[...].astype(F32))).astype(BF16)

    y = jnp.dot(act_scr[...], w_ref[...], preferred_element_type=F32)
    o_ref[...] = x_ref[...] + gt_ref[...] * y


def _conv_out(v, z, ln_g, ln_b, w_out, x, mods, *, mod_row, tm, tn):
    bsz, seq_len, _ = x.shape
    row = (lambda b: b) if mod_row is None else (lambda b: mod_row)
    return pl.pallas_call(
        _conv_out_kernel,
        grid=(bsz, seq_len // tm, D_MODEL // tn),
        in_specs=[
            pl.BlockSpec((None, tm, E_WIDE), lambda b, i, j: (b, i, 0)),
            pl.BlockSpec((None, tm, E_WIDE), lambda b, i, j: (b, i, 0)),
            pl.BlockSpec((1, E_WIDE), lambda b, i, j: (0, 0)),
            pl.BlockSpec((1, E_WIDE), lambda b, i, j: (0, 0)),
            pl.BlockSpec((E_WIDE, tn), lambda b, i, j: (0, j)),
            pl.BlockSpec((None, tm, tn), lambda b, i, j: (b, i, j)),
            pl.BlockSpec((None, None, 1, tn), lambda b, i, j: (row(b), 2, 0, j)),
        ],
        out_specs=pl.BlockSpec((None, tm, tn), lambda b, i, j: (b, i, j)),
        out_shape=jax.ShapeDtypeStruct(x.shape, F32),
        scratch_shapes=[pltpu.VMEM((tm, E_WIDE), BF16)],
        compiler_params=_params("parallel", "parallel", "arbitrary"),
        name="conv_out",
    )(v, z, ln_g, ln_b, w_out, x, mods)


POOL_ROWS = 64


def _pool_in_kernel(xp_ref, x_ref, xn_ref, sh_ref, sc_ref, g_ref, wu_ref, wz_ref, wg_ref,
                    ps_ref, o_ref, h_scr, u_scr, p_scr, *, tm, seq_len):
    i, grp = pl.program_id(1), pl.program_id(2)
    tn = POOL_GROUP

    @pl.when(grp == 0)
    def _():
        _fill_h(h_scr, xp_ref, x_ref, xn_ref, sh_ref, sc_ref, g_ref, tm)

    u = jnp.dot(h_scr[...], wu_ref[...], preferred_element_type=F32)
    u_scr[...] = jnp.where(_seq_row_mask(i, tm, seq_len, tn), u, 0.0)

    for gi, win in enumerate(POOL_WINDOWS):
        @pl.when(grp == gi)
        def _(win=win):
            half = win // 2

            for r0 in range(0, tm, POOL_ROWS):
                t = lax.broadcasted_iota(jnp.int32, (POOL_ROWS, 1), 0) + (i * tm + r0)
                cnt = jnp.minimum(t + half, seq_len) - jnp.maximum(t - half, 0)
                acc = u_scr[pl.ds(r0 + (HALO - half), POOL_ROWS), :]
                for d in range(1, win):
                    acc = acc + u_scr[pl.ds(r0 + (HALO - half + d), POOL_ROWS), :]
                p = acc / cnt.astype(F32) - u_scr[pl.ds(r0 + HALO, POOL_ROWS), :]
                p_scr[pl.ds(r0, POOL_ROWS), :] = p.astype(BF16)

    z = jnp.dot(h_scr[HALO:HALO + tm, :], wz_ref[...], preferred_element_type=F32)
    q = jnp.dot(p_scr[...], wg_ref[...], preferred_element_type=F32)
    o_ref[...] = (q * ps_ref[...] * _silu(z)).astype(o_ref.dtype)


def _pool_in(x, mods, norm_g, w_in, w_grp, scale, *, mod_row, tm):
    bsz, seq_len, _ = x.shape
    ng = len(POOL_WINDOWS)
    tn = POOL_GROUP
    kern = functools.partial(_pool_in_kernel, tm=tm, seq_len=seq_len)
    return pl.pallas_call(
        kern,
        grid=(bsz, seq_len // tm, ng),
        in_specs=_x_specs(tm, seq_len, mod_row) + [
            pl.BlockSpec((D_MODEL, tn), lambda b, i, j: (0, j)),
            pl.BlockSpec((D_MODEL, tn), lambda b, i, j: (0, ng + j)),
            pl.BlockSpec((None, tn, tn), lambda b, i, j: (j, 0, 0)),
            pl.BlockSpec((1, tn), lambda b, i, j: (0, j)),
        ],
        out_specs=pl.BlockSpec((None, tm, tn), lambda b, i, j: (b, i, j)),
        out_shape=jax.ShapeDtypeStruct((bsz, seq_len, E_WIDE), BF16),
        scratch_shapes=[pltpu.VMEM((tm + 2 * HALO, D_MODEL), BF16),
                        pltpu.VMEM((tm + 2 * HALO, tn), F32),
                        pltpu.VMEM((tm, tn), BF16)],
        compiler_params=_params("parallel", "parallel", "arbitrary"),
        name="pool_in",
    )(x, x, x, mods, mods, norm_g, w_in, w_in, w_grp, scale)


def _proj_out_kernel(a_ref, w_ref, x_ref, gt_ref, o_ref):
    y = jnp.dot(a_ref[...], w_ref[...], preferred_element_type=F32)
    o_ref[...] = x_ref[...] + gt_ref[...] * y


def _proj_out(act, w_out, x, mods, *, mod_row, tm, tn):
    bsz, seq_len, _ = x.shape
    kdim = act.shape[-1]
    row = (lambda b: b) if mod_row is None else (lambda b: mod_row)
    return pl.pallas_call(
        _proj_out_kernel,
        grid=(bsz, seq_len // tm, D_MODEL // tn),
        in_specs=[
            pl.BlockSpec((None, tm, kdim), lambda b, i, j: (b, i, 0)),
            pl.BlockSpec((kdim, tn), lambda b, i, j: (0, j)),
            pl.BlockSpec((None, tm, tn), lambda b, i, j: (b, i, j)),
            pl.BlockSpec((None, None, 1, tn), lambda b, i, j: (row(b), 2, 0, j)),
        ],
        out_specs=pl.BlockSpec((None, tm, tn), lambda b, i, j: (b, i, j)),
        out_shape=jax.ShapeDtypeStruct(x.shape, F32),
        compiler_params=_params("parallel", "parallel", "arbitrary"),
        name="proj_out",
    )(act, w_out, x, mods)


QUERY_SCALE = ATT_HEAD_DIM ** -0.5 * math.log2(math.e)
FLASH_ROWS = 128


def _rope_tables(seq_len):
    t = jnp.arange(seq_len)
    n_freq = ATT_HEAD_DIM // 4
    freqs = ROPE_BASE ** (-jnp.arange(n_freq, dtype=F32) / n_freq)
    pos = jnp.stack([t // GRID_W, t % GRID_W], axis=1).astype(F32)
    ang = pos[:, :, None] * freqs
    cos, sin = jnp.cos(ang), jnp.sin(ang)
    zero = jnp.zeros_like(sin)

    def table(first, second):
        tab = jnp.stack([first, second], axis=2)
        return jnp.tile(tab.reshape(seq_len, ATT_HEAD_DIM), (1, 2))

    return table(cos, cos), table(-sin, zero), table(zero, sin)


def _att_in_kernel(xp_ref, x_ref, xn_ref, sh_ref, sc_ref, g_ref, w_ref, cos_ref, sa_ref, sb_ref,
                   o_ref, h_scr, *, tm, tn, n_q_tiles, n_rope_tiles):
    del xp_ref, xn_ref
    j = pl.program_id(2)

    @pl.when(j == 0)
    def _():
        h_scr[...] = _norm_mod(x_ref[...], g_ref[...], sc_ref[...], sh_ref[...])

    y = jnp.dot(h_scr[...], w_ref[...], preferred_element_type=F32)

    @pl.when(j < n_rope_tiles)
    def _():
        scale = jnp.where(j < n_q_tiles, QUERY_SCALE, 1.0)
        cos, sa, sb = cos_ref[...], sa_ref[...], sb_ref[...]
        half = ATT_HEAD_DIM // 4
        for c in range(tn // LANES):
            yc = y[:, c * LANES:(c + 1) * LANES]
            r = (yc * cos + pltpu.roll(yc, LANES - half, axis=1) * sa
                 + pltpu.roll(yc, half, axis=1) * sb)
            o_ref[:, c * LANES:(c + 1) * LANES] = (r * scale).astype(o_ref.dtype)

    @pl.when(j >= n_rope_tiles)
    def _():
        o_ref[...] = y.astype(o_ref.dtype)


def _att_in(x, mods, norm_g, w_in, *, mod_row, tm, tn, col_lo, n_cols, rope):
    bsz, seq_len, _ = x.shape
    j0 = col_lo // tn
    if rope:
        tabs = _rope_tables(seq_len)
        n_q, n_rope = ATT_WIDTH // tn, 2 * ATT_WIDTH // tn
    else:
        tabs = (jnp.zeros((seq_len, LANES), F32),) * 3
        n_q = n_rope = 0
    kern = functools.partial(_att_in_kernel, tm=tm, tn=tn, n_q_tiles=n_q, n_rope_tiles=n_rope)
    tab_spec = pl.BlockSpec((tm, LANES), lambda b, i, j: (i, 0))
    return pl.pallas_call(
        kern,
        grid=(bsz, seq_len // tm, n_cols // tn),
        in_specs=_x_specs(tm, seq_len, mod_row) + [
            pl.BlockSpec((D_MODEL, tn), lambda b, i, j: (0, j0 + j)),
            tab_spec, tab_spec, tab_spec,
        ],
        out_specs=pl.BlockSpec((None, tm, tn), lambda b, i, j: (b, i, j)),
        out_shape=jax.ShapeDtypeStruct((bsz, seq_len, n_cols), BF16),
        scratch_shapes=[pltpu.VMEM((tm, D_MODEL), BF16)],
        compiler_params=_params("parallel", "parallel", "arbitrary"),
        name="att_in",
    )(x, x, x, mods, mods, norm_g, w_in, *tabs)


def _flash_kernel(lq1_ref, lk1_ref, lq2_ref, lk2_ref, sg_ref, q_ref, k_ref, v_ref, kc_ref, vc_ref,
                  z_ref, y_ref, m_scr, l_scr, acc_scr, s_scr, p_scr, a_scr, sc_scr, pc_scr,
                  *, tq, tk, seq_len, lambda_init):
    q = q_ref[...]
    lane = lax.broadcasted_iota(jnp.int32, q.shape, 1)
    zero = jnp.zeros_like(q)
    qq = jnp.concatenate([jnp.where(lane < ATT_HEAD_DIM, q, zero),
                          jnp.where(lane >= ATT_HEAD_DIM, q, zero)], axis=0)

    m_scr[...] = jnp.full(m_scr.shape, -jnp.inf, F32)
    l_scr[...] = jnp.zeros(l_scr.shape, F32)
    acc_scr[...] = jnp.zeros(acc_scr.shape, F32)
    n_chunks = seq_len // tk
    nt = (((1,), (1,)), ((), ()))

    def scores(kb):
        return lax.dot_general(qq, kb, nt, preferred_element_type=F32)

    def softmax(s_ref, p_ref, a_ref, width):
        for r in range(0, 2 * tq, FLASH_ROWS):
            rows = slice(r, r + FLASH_ROWS)
            cols = [s_ref[rows, j:j + LANES] for j in range(0, width, LANES)]
            part = functools.reduce(jnp.maximum, cols)
            m_old = m_scr[rows, :]
            m_new = jnp.maximum(m_old, jnp.max(part, axis=1, keepdims=True))
            alpha = jnp.exp2(m_old - m_new)
            ps = [jnp.exp2(cj - m_new) for cj in cols]
            l_scr[rows, :] = alpha * l_scr[rows, :] + functools.reduce(jnp.add, ps)
            m_scr[rows, :] = m_new
            a_ref[rows, :] = alpha
            for j, pj in enumerate(ps):
                p_ref[rows, j * LANES:(j + 1) * LANES] = pj.astype(BF16)

    def accumulate(p_ref, a_ref, vb):
        acc_scr[...] = a_ref[...] * acc_scr[...] + jnp.dot(p_ref[...], vb, preferred_element_type=F32)

    def k_chunk(c):
        return k_ref[pl.ds(pl.multiple_of(c * tk, tk), tk), :]

    def v_chunk(c):
        return v_ref[pl.ds(pl.multiple_of(c * tk, tk), tk), :]

    sc_scr[...] = scores(kc_ref[...])
    softmax(sc_scr, pc_scr, a_scr.at[0], kc_ref.shape[0])
    accumulate(pc_scr, a_scr.at[0], vc_ref[...])

    s_scr[0] = scores(k_chunk(0))
    softmax(s_scr.at[0], p_scr.at[0], a_scr.at[0], tk)
    s_scr[1] = scores(k_chunk(1))

    def stage(c, cur, prv, prefetch):
        if prefetch:
            s_scr[prv] = scores(k_chunk(c + 1))
        accumulate(p_scr.at[prv], a_scr.at[prv], v_chunk(c - 1))
        softmax(s_scr.at[cur], p_scr.at[cur], a_scr.at[cur], tk)

    def body(i, carry):
        stage(2 * i + 1, 1, 0, True)
        stage(2 * i + 2, 0, 1, True)
        return carry

    lax.fori_loop(0, n_chunks // 2 - 1, body, 0)
    stage(n_chunks - 1, 1, 0, False)
    accumulate(p_scr.at[1], a_scr.at[1], v_chunk(n_chunks - 1))

    lam = (jnp.exp(jnp.sum(lq1_ref[...] * lk1_ref[...], axis=1, keepdims=True))
           - jnp.exp(jnp.sum(lq2_ref[...] * lk2_ref[...], axis=1, keepdims=True)) + lambda_init)
    l1 = jnp.sum(l_scr[0:tq, :], axis=1, keepdims=True)
    l2 = jnp.sum(l_scr[tq:, :], axis=1, keepdims=True)
    o = acc_scr[0:tq, :] / l1 - lam * (acc_scr[tq:, :] / l2)
    o = o * lax.rsqrt(jnp.mean(o * o, axis=-1, keepdims=True) + SUBLN_EPS)
    o = (o * sg_ref[...]) * (1.0 - lambda_init)
    y_ref[...] = (o * _silu(z_ref[...].astype(F32))).astype(y_ref.dtype)


def _flash(qkvz, kvc, lam_params, subln_g, *, lambda_init, tq, tk):
    bsz, seq_len, _ = qkvz.shape
    ctx_len = kvc.shape[1]
    nh = N_ATT_HEADS
    assert seq_len % (2 * tk) == 0 and ctx_len % LANES == 0
    kern = functools.partial(_flash_kernel, tq=tq, tk=tk, seq_len=seq_len, lambda_init=lambda_init)
    vec = pl.BlockSpec((1, ATT_HEAD_DIM), lambda b, h, i: (0, 0))
    return pl.pallas_call(
        kern,
        grid=(bsz, nh, seq_len // tq),
        in_specs=[vec, vec, vec, vec,
                  pl.BlockSpec((1, HEAD_W), lambda b, h, i: (0, 0)),
                  pl.BlockSpec((None, tq, HEAD_W), lambda b, h, i: (b, i, h)),
                  pl.BlockSpec((None, seq_len, HEAD_W), lambda b, h, i: (b, 0, nh + h)),
                  pl.BlockSpec((None, seq_len, HEAD_W), lambda b, h, i: (b, 0, 2 * nh + h)),
                  pl.BlockSpec((None, ctx_len, HEAD_W), lambda b, h, i: (b, 0, h)),
                  pl.BlockSpec((None, ctx_len, HEAD_W), lambda b, h, i: (b, 0, nh + h)),
                  pl.BlockSpec((None, tq, HEAD_W), lambda b, h, i: (b, i, 3 * nh + h))],
        out_specs=pl.BlockSpec((None, tq, HEAD_W), lambda b, h, i: (b, i, h)),
        out_shape=jax.ShapeDtypeStruct((bsz, seq_len, ATT_WIDTH), BF16),
        scratch_shapes=[pltpu.VMEM((2 * tq, LANES), F32), pltpu.VMEM((2 * tq, LANES), F32),
                        pltpu.VMEM((2 * tq, HEAD_W), F32),
                        pltpu.VMEM((2, 2 * tq, tk), F32), pltpu.VMEM((2, 2 * tq, tk), BF16),
                        pltpu.VMEM((2, 2 * tq, LANES), F32),
                        pltpu.VMEM((2 * tq, ctx_len), F32), pltpu.VMEM((2 * tq, ctx_len), BF16)],
        compiler_params=_params("parallel", "parallel", "arbitrary"),
        name="diff_flash",
    )(*lam_params, subln_g, qkvz, qkvz, qkvz, kvc, kvc, qkvz)


def _final_norm_kernel(x_ref, g_ref, o_ref):
    xv = x_ref[...]
    o_ref[...] = xv * lax.rsqrt(jnp.mean(xv * xv, axis=-1, keepdims=True) + EPS) * g_ref[...]


def _final_norm(x, g, *, tm):
    bsz, seq_len, _ = x.shape
    return pl.pallas_call(
        _final_norm_kernel,
        grid=(bsz, seq_len // tm),
        in_specs=[pl.BlockSpec((None, tm, D_MODEL), lambda b, i: (b, i, 0)),
                  pl.BlockSpec((1, D_MODEL), lambda b, i: (0, 0))],
        out_specs=pl.BlockSpec((None, tm, D_MODEL), lambda b, i: (b, i, 0)),
        out_shape=jax.ShapeDtypeStruct(x.shape, F32),
        compiler_params=_params("parallel", "parallel"),
        name="final_norm",
    )(x, g)


def kernel(x, c, ctx, c_ctx, norm_g, w_mod, b_mod, final_g, conv_w_in, conv_dw_w, conv_dw_b,
           conv_ln_g, conv_ln_b, conv_w_out, pool_w_in, pool_w_grp, pool_scale, pool_w_out,
           att_w_in, att_lam_q1, att_lam_k1, att_lam_q2, att_lam_k2, att_subln_g, att_w_out):
    bsz = x.shape[0]
    ctx_len = ctx.shape[1]
    n_att = DEPTH // N_MIXERS
    last_ctx_reader = N_MIXERS * (n_att - 1) + ATT_ID if n_att > 0 else -1
    ctx_row = bsz

    cc = jnp.zeros((MOD_ROWS, D_MODEL), F32).at[:bsz].set(c).at[ctx_row].set(c_ctx)
    mods_all = _modulation(cc, w_mod, b_mod).reshape(DEPTH, MOD_ROWS, 3, 1, D_MODEL)

    tm, tn = 512, 512
    xs, cs = x, ctx
    for i in range(DEPTH):
        kind, j = i % N_MIXERS, i // N_MIXERS
        ctx_live = i <= last_ctx_reader
        ctx_update = i < last_ctx_reader
        mods = mods_all[i]
        g = norm_g[i].reshape(1, D_MODEL)
        streams = [(xs, None, tm)]
        if ctx_update:
            streams.append((cs, ctx_row, ctx_len))
        new = []
        if kind == CONV_ID:
            w_in = conv_w_in[j].astype(BF16)
            w_out = conv_w_out[j].astype(BF16)
            for s, row, tms in streams:
                v, z = _conv_in(s, mods, g, w_in, conv_dw_w[j], conv_dw_b[j].reshape(1, E_WIDE),
                                mod_row=row, tm=tms, tn=tn)
                new.append(_conv_out(v, z, conv_ln_g[j].reshape(1, E_WIDE),
                                     conv_ln_b[j].reshape(1, E_WIDE), w_out, s, mods,
                                     mod_row=row, tm=min(tms, 256), tn=tn))
        elif kind == POOL_ID:
            w_in = pool_w_in[j].astype(BF16)
            w_grp = pool_w_grp[j].astype(BF16)
            w_out = pool_w_out[j].astype(BF16)
            for s, row, tms in streams:
                act = _pool_in(s, mods, g, w_in, w_grp, pool_scale[j].reshape(1, E_WIDE),
                               mod_row=row, tm=tms)
                new.append(_proj_out(act, w_out, s, mods, mod_row=row, tm=tms, tn=tn))
        else:
            lambda_init = 0.8 - 0.6 * math.exp(-0.3 * i)
            w_in = att_w_in[j].astype(BF16)
            w_out = att_w_out[j].astype(BF16)
            qkvz = _att_in(xs, mods, g, w_in, mod_row=None, tm=tm, tn=tn,
                           col_lo=0, n_cols=4 * ATT_WIDTH, rope=True)
            kvc = _att_in(cs, mods, g, w_in, mod_row=ctx_row, tm=ctx_len, tn=tn,
                          col_lo=ATT_WIDTH, n_cols=2 * ATT_WIDTH, rope=False)
            lam_params = [p[j].reshape(1, ATT_HEAD_DIM)
                          for p in (att_lam_q1, att_lam_k1, att_lam_q2, att_lam_k2)]
            y = _flash(qkvz, kvc, lam_params, att_subln_g[j].reshape(1, HEAD_W),
                       lambda_init=lambda_init, tq=512, tk=512)
            new.append(_proj_out(y, w_out, xs, mods, mod_row=None, tm=tm, tn=tn))
            if ctx_update:
                raise NotImplementedError("context-side queries are not needed at this depth")
        xs = new[0]
        if ctx_update:
            cs = new[1]
        del ctx_live
    return _final_norm(xs, final_g.reshape(1, D_MODEL), tm=tm)
```

```python
import functools
import math

import jax
import jax.numpy as jnp
from jax import lax
from jax.experimental import pallas as pl
from jax.experimental.pallas import tpu as pltpu

D_MODEL = 2048
DEPTH = 4
GRID_W = 64
N_MIXERS = 3
CONV_ID, POOL_ID, ATT_ID = 0, 1, 2
E_WIDE = 2 * D_MODEL
CONV_K = 31
CONV_HALF = CONV_K // 2
POOL_WINDOWS = (2, 4, 8, 16)
POOL_GROUP = E_WIDE // len(POOL_WINDOWS)
ATT_WIDTH = D_MODEL
ATT_HEAD_DIM = 64
HEAD_W = 2 * ATT_HEAD_DIM
N_ATT_HEADS = ATT_WIDTH // HEAD_W
ROPE_BASE = 10000.0
EPS = 1e-6
SUBLN_EPS = 1e-5

LANES = 128
HALO = 16
MOD_ROWS = 8
VMEM_LIMIT = 56 * 1024 * 1024

F32 = jnp.float32
BF16 = jnp.bfloat16


def _silu(v):
    return v * jax.nn.sigmoid(v)


def _params(*sem, flags=None):
    return pltpu.CompilerParams(dimension_semantics=sem, vmem_limit_bytes=VMEM_LIMIT, flags=flags)


def _norm_mod(xv, g, sc, sh):
    xv = xv.astype(F32)
    y = xv * lax.rsqrt(jnp.mean(xv * xv, axis=-1, keepdims=True) + EPS)
    return ((y * g) * (1.0 + sc) + sh).astype(BF16)


def _mod_kernel(c_ref, w_ref, b_ref, o_ref):
    s = _silu(c_ref[...])
    o_ref[...] = jnp.dot(s, w_ref[...], precision=lax.Precision.HIGHEST,
                         preferred_element_type=F32) + b_ref[...]


def _modulation(cc, w_mod, b_mod):
    tn = 1024
    n = 3 * D_MODEL
    return pl.pallas_call(
        _mod_kernel,
        grid=(DEPTH, n // tn),
        in_specs=[
            pl.BlockSpec((MOD_ROWS, D_MODEL), lambda l, j: (0, 0)),
            pl.BlockSpec((None, D_MODEL, tn), lambda l, j: (l, 0, j)),
            pl.BlockSpec((None, 1, tn), lambda l, j: (l, 0, j)),
        ],
        out_specs=pl.BlockSpec((None, MOD_ROWS, tn), lambda l, j: (l, 0, j)),
        out_shape=jax.ShapeDtypeStruct((DEPTH, MOD_ROWS, n), F32),
        compiler_params=_params("parallel", "parallel"),
        name="modulation",
    )(cc, w_mod, b_mod.reshape(DEPTH, 1, n))


def _x_specs(tm, seq_len, mod_row):
    nh = tm // HALO
    last = seq_len // HALO - 1
    row = (lambda b: b) if mod_row is None else (lambda b: mod_row)
    return [
        pl.BlockSpec((None, HALO, D_MODEL), lambda b, i, j: (b, jnp.maximum(i * nh - 1, 0), 0)),
        pl.BlockSpec((None, tm, D_MODEL), lambda b, i, j: (b, i, 0)),
        pl.BlockSpec((None, HALO, D_MODEL), lambda b, i, j: (b, jnp.minimum((i + 1) * nh, last), 0)),
        pl.BlockSpec((None, None, 1, D_MODEL), lambda b, i, j: (row(b), 0, 0, 0)),
        pl.BlockSpec((None, None, 1, D_MODEL), lambda b, i, j: (row(b), 1, 0, 0)),
        pl.BlockSpec((1, D_MODEL), lambda b, i, j: (0, 0)),
    ]


def _fill_h(h_scr, xp_ref, x_ref, xn_ref, sh_ref, sc_ref, g_ref, tm):
    g, sc, sh = g_ref[...], sc_ref[...], sh_ref[...]
    h_scr[0:HALO, :] = _norm_mod(xp_ref[...], g, sc, sh)
    h_scr[HALO:HALO + tm, :] = _norm_mod(x_ref[...], g, sc, sh)
    h_scr[HALO + tm:, :] = _norm_mod(xn_ref[...], g, sc, sh)


def _seq_row_mask(i, tm, seq_len, tn):
    row = lax.broadcasted_iota(jnp.int32, (tm + 2 * HALO, tn), 0) + (i * tm - HALO)
    return (row >= 0) & (row < seq_len)


SUB = 8
CONV_GROUPS = 4
SHIFT_ROWS = 64


def _conv_in_kernel(xp_ref, x_ref, xn_ref, sh_ref, sc_ref, g_ref, wa_ref, wb_ref, wz_ref,
                    dw_ref, db_ref, v_ref, z_ref, h_scr, u_scr, us_scr, wt_scr, *, tm, tn, seq_len):
    i, j = pl.program_id(1), pl.program_id(2)
    ext = tm + 2 * HALO
    n_grp = ext // SUB
    hw = tn // 2

    @pl.when(j == 0)
    def _():
        _fill_h(h_scr, xp_ref, x_ref, xn_ref, sh_ref, sc_ref, g_ref, tm)

    for k in range(CONV_K):
        wt_scr[k] = jnp.broadcast_to(dw_ref[k:k + 1, :], (SUB, tn))
    bias = jnp.broadcast_to(db_ref[...], (SUB, tn))
    mask = _seq_row_mask(i, tm, seq_len, hw)
    h = h_scr[...]
    halves = (slice(0, hw), slice(hw, tn))

    def glu(cs):
        a = jnp.dot(h, wa_ref[:, cs], preferred_element_type=F32)
        b = jnp.dot(h, wb_ref[:, cs], preferred_element_type=F32)
        u = jnp.where(mask, a * jax.nn.sigmoid(b), 0.0)
        u_scr[:, cs] = u
        us_scr[0, :, :, cs] = u.reshape(n_grp, SUB, hw)

    def conv(cs):
        for r in range(1, SUB):
            for c0 in range(0, ext - SUB, SHIFT_ROWS):
                n = min(SHIFT_ROWS, ext - SUB - c0)
                us_scr[r, c0 // SUB:(c0 + n) // SUB, :, cs] = (
                    u_scr[pl.ds(c0 + r, n), cs].reshape(n // SUB, SUB, hw))
        for g0 in range(0, tm // SUB, CONV_GROUPS):
            acc = jnp.broadcast_to(bias[:, cs], (CONV_GROUPS, SUB, hw))
            for k in range(CONV_K):
                off = HALO - CONV_HALF + k
                lo = g0 + off // SUB
                acc = acc + us_scr[off % SUB, lo:lo + CONV_GROUPS, :, cs] * wt_scr[k, :, cs]
            v_ref[g0 * SUB:(g0 + CONV_GROUPS) * SUB, cs] = (
                acc.reshape(CONV_GROUPS * SUB, hw).astype(v_ref.dtype))

    glu(halves[0])
    glu(halves[1])
    conv(halves[0])
    z_ref[...] = jnp.dot(h_scr[HALO:HALO + tm, :], wz_ref[...],
                         preferred_element_type=F32).astype(z_ref.dtype)
    conv(halves[1])


def _conv_in(x, mods, norm_g, w_in, dw_w, dw_b, *, mod_row, tm, tn):
    bsz, seq_len, _ = x.shape
    nj = E_WIDE // tn
    kern = functools.partial(_conv_in_kernel, tm=tm, tn=tn, seq_len=seq_len)
    out = jax.ShapeDtypeStruct((bsz, seq_len, E_WIDE), BF16)
    return pl.pallas_call(
        kern,
        grid=(bsz, seq_len // tm, nj),
        in_specs=_x_specs(tm, seq_len, mod_row) + [
            pl.BlockSpec((D_MODEL, tn), lambda b, i, j: (0, j)),
            pl.BlockSpec((D_MODEL, tn), lambda b, i, j: (0, nj + j)),
            pl.BlockSpec((D_MODEL, tn), lambda b, i, j: (0, 2 * nj + j)),
            pl.BlockSpec((CONV_K, tn), lambda b, i, j: (0, j)),
            pl.BlockSpec((1, tn), lambda b, i, j: (0, j)),
        ],
        out_specs=[pl.BlockSpec((None, tm, tn), lambda b, i, j: (b, i, j))] * 2,
        out_shape=[out, out],
        scratch_shapes=[pltpu.VMEM((tm + 2 * HALO, D_MODEL), BF16),
                        pltpu.VMEM((tm + 2 * HALO, tn), F32),
                        pltpu.VMEM((SUB, (tm + 2 * HALO) // SUB, SUB, tn), F32),
                        pltpu.VMEM((CONV_K, SUB, tn), F32)],
        compiler_params=_params("parallel", "parallel", "arbitrary"),
        name="conv_in",
    )(x, x, x, mods, mods, norm_g, w_in, w_in, w_in, dw_w, dw_b)


LN_ROWS = 16
OUT_KC = 1024


def _rmsnorm_rows(xv, g):
    return xv * lax.rsqrt(jnp.mean(xv * xv, axis=-1, keepdims=True) + EPS) * g


def _conv_out_kernel(v_ref, z_ref, lg_ref, lb_ref, w_ref, x_ref, gt_ref, fg_ref, o_ref, *, tm, final_norm):
    mus, rss = [], []
    for r in range(0, tm, LN_ROWS):
        v = v_ref[r:r + LN_ROWS, :].astype(F32)
        mu = jnp.mean(v, axis=-1, keepdims=True)
        d = v - mu
        mus.append(mu)
        rss.append(lax.rsqrt(jnp.mean(d * d, axis=-1, keepdims=True) + EPS))
    mu, rs = jnp.concatenate(mus, axis=0), jnp.concatenate(rss, axis=0)

    acc = None
    for k0 in range(0, E_WIDE, OUT_KC):
        ks = slice(k0, k0 + OUT_KC)
        y = _silu((v_ref[:, ks].astype(F32) - mu) * rs * lg_ref[:, ks] + lb_ref[:, ks])
        act = (y * _silu(z_ref[:, ks].astype(F32))).astype(BF16)
        part = jnp.dot(act, w_ref[ks, :], preferred_element_type=F32)
        acc = part if acc is None else acc + part
    o = x_ref[...] + gt_ref[...] * acc
    o_ref[...] = _rmsnorm_rows(o, fg_ref[...]) if final_norm else o


def _resident(shape):
    return pl.BlockSpec(shape, lambda *_: (0,) * len(shape), pipeline_mode=pl.Buffered(1))


def _conv_out(v, z, ln_g, ln_b, w_out, x, mods, final_g, *, mod_row, tm, final_norm):
    bsz, seq_len, _ = x.shape
    row = (lambda b: b) if mod_row is None else (lambda b: mod_row)
    kern = functools.partial(_conv_out_kernel, tm=tm, final_norm=final_norm)
    return pl.pallas_call(
        kern,
        grid=(bsz, seq_len // tm),
        in_specs=[
            pl.BlockSpec((None, tm, E_WIDE), lambda b, i: (b, i, 0)),
            pl.BlockSpec((None, tm, E_WIDE), lambda b, i: (b, i, 0)),
            _resident((1, E_WIDE)),
            _resident((1, E_WIDE)),
            _resident((E_WIDE, D_MODEL)),
            pl.BlockSpec((None, tm, D_MODEL), lambda b, i: (b, i, 0)),
            pl.BlockSpec((None, None, 1, D_MODEL), lambda b, i: (row(b), 2, 0, 0)),
            _resident((1, D_MODEL)),
        ],
        out_specs=pl.BlockSpec((None, tm, D_MODEL), lambda b, i: (b, i, 0)),
        out_shape=jax.ShapeDtypeStruct(x.shape, F32),
        compiler_params=_params("parallel", "parallel"),
        name="conv_out",
    )(v, z, ln_g, ln_b, w_out, x, mods, final_g)


POOL_ROWS = 64


def _pool_in_kernel(xp_ref, x_ref, xn_ref, sh_ref, sc_ref, g_ref, wu_ref, wz_ref, wg_ref,
                    ps_ref, o_ref, h_scr, u_scr, p_scr, *, tm, seq_len):
    i, grp = pl.program_id(1), pl.program_id(2)
    tn = POOL_GROUP

    @pl.when(grp == 0)
    def _():
        _fill_h(h_scr, xp_ref, x_ref, xn_ref, sh_ref, sc_ref, g_ref, tm)

    u = jnp.dot(h_scr[...], wu_ref[...], preferred_element_type=F32)
    u_scr[...] = jnp.where(_seq_row_mask(i, tm, seq_len, tn), u, 0.0)

    for gi, win in enumerate(POOL_WINDOWS):
        @pl.when(grp == gi)
        def _(win=win):
            half = win // 2

            for r0 in range(0, tm, POOL_ROWS):
                t = lax.broadcasted_iota(jnp.int32, (POOL_ROWS, 1), 0) + (i * tm + r0)
                cnt = jnp.minimum(t + half, seq_len) - jnp.maximum(t - half, 0)
                acc = u_scr[pl.ds(r0 + (HALO - half), POOL_ROWS), :]
                for d in range(1, win):
                    acc = acc + u_scr[pl.ds(r0 + (HALO - half + d), POOL_ROWS), :]
                p = acc / cnt.astype(F32) - u_scr[pl.ds(r0 + HALO, POOL_ROWS), :]
                p_scr[pl.ds(r0, POOL_ROWS), :] = p.astype(BF16)

    z = jnp.dot(h_scr[HALO:HALO + tm, :], wz_ref[...], preferred_element_type=F32)
    q = jnp.dot(p_scr[...], wg_ref[...], preferred_element_type=F32)
    o_ref[...] = (q * ps_ref[...] * _silu(z)).astype(o_ref.dtype)


def _pool_in(x, mods, norm_g, w_in, w_grp, scale, *, mod_row, tm):
    bsz, seq_len, _ = x.shape
    ng = len(POOL_WINDOWS)
    tn = POOL_GROUP
    kern = functools.partial(_pool_in_kernel, tm=tm, seq_len=seq_len)
    return pl.pallas_call(
        kern,
        grid=(bsz, seq_len // tm, ng),
        in_specs=_x_specs(tm, seq_len, mod_row) + [
            pl.BlockSpec((D_MODEL, tn), lambda b, i, j: (0, j)),
            pl.BlockSpec((D_MODEL, tn), lambda b, i, j: (0, ng + j)),
            pl.BlockSpec((None, tn, tn), lambda b, i, j: (j, 0, 0)),
            pl.BlockSpec((1, tn), lambda b, i, j: (0, j)),
        ],
        out_specs=pl.BlockSpec((None, tm, tn), lambda b, i, j: (b, i, j)),
        out_shape=jax.ShapeDtypeStruct((bsz, seq_len, E_WIDE), BF16),
        scratch_shapes=[pltpu.VMEM((tm + 2 * HALO, D_MODEL), BF16),
                        pltpu.VMEM((tm + 2 * HALO, tn), F32),
                        pltpu.VMEM((tm, tn), BF16)],
        compiler_params=_params("parallel", "parallel", "arbitrary"),
        name="pool_in",
    )(x, x, x, mods, mods, norm_g, w_in, w_in, w_grp, scale)


def _proj_out_kernel(a_ref, w_ref, x_ref, gt_ref, o_ref):
    y = jnp.dot(a_ref[...], w_ref[...], preferred_element_type=F32)
    o_ref[...] = x_ref[...] + gt_ref[...] * y


def _proj_out(act, w_out, x, mods, *, mod_row, tm):
    bsz, seq_len, _ = x.shape
    kdim = act.shape[-1]
    row = (lambda b: b) if mod_row is None else (lambda b: mod_row)
    return pl.pallas_call(
        _proj_out_kernel,
        grid=(bsz, seq_len // tm),
        in_specs=[
            pl.BlockSpec((None, tm, kdim), lambda b, i: (b, i, 0)),
            _resident((kdim, D_MODEL)),
            pl.BlockSpec((None, tm, D_MODEL), lambda b, i: (b, i, 0)),
            pl.BlockSpec((None, None, 1, D_MODEL), lambda b, i: (row(b), 2, 0, 0)),
        ],
        out_specs=pl.BlockSpec((None, tm, D_MODEL), lambda b, i: (b, i, 0)),
        out_shape=jax.ShapeDtypeStruct(x.shape, F32),
        compiler_params=_params("parallel", "parallel"),
        name="proj_out",
    )(act, w_out, x, mods)


QUERY_SCALE = ATT_HEAD_DIM ** -0.5 * math.log2(math.e)
FLASH_ROWS = 32


def _rope_tables(seq_len):
    t = jnp.arange(seq_len)
    n_freq = ATT_HEAD_DIM // 4
    freqs = ROPE_BASE ** (-jnp.arange(n_freq, dtype=F32) / n_freq)
    pos = jnp.stack([t // GRID_W, t % GRID_W], axis=1).astype(F32)
    ang = pos[:, :, None] * freqs
    cos, sin = jnp.cos(ang), jnp.sin(ang)
    zero = jnp.zeros_like(sin)

    def table(first, second):
        tab = jnp.stack([first, second], axis=2)
        return jnp.tile(tab.reshape(seq_len, ATT_HEAD_DIM), (1, 2))

    return table(cos, cos), table(-sin, zero), table(zero, sin)


def _att_in_kernel(xp_ref, x_ref, xn_ref, sh_ref, sc_ref, g_ref, w_ref, cos_ref, sa_ref, sb_ref,
                   o_ref, h_scr, *, tm, tn, n_q_tiles, n_rope_tiles):
    del xp_ref, xn_ref
    j = pl.program_id(2)

    @pl.when(j == 0)
    def _():
        h_scr[...] = _norm_mod(x_ref[...], g_ref[...], sc_ref[...], sh_ref[...])

    y = jnp.dot(h_scr[...], w_ref[...], preferred_element_type=F32)

    @pl.when(j < n_rope_tiles)
    def _():
        scale = jnp.where(j < n_q_tiles, QUERY_SCALE, 1.0)
        cos, sa, sb = cos_ref[...], sa_ref[...], sb_ref[...]
        half = ATT_HEAD_DIM // 4
        for c in range(tn // LANES):
            yc = y[:, c * LANES:(c + 1) * LANES]
            r = (yc * cos + pltpu.roll(yc, LANES - half, axis=1) * sa
                 + pltpu.roll(yc, half, axis=1) * sb)
            o_ref[:, c * LANES:(c + 1) * LANES] = (r * scale).astype(o_ref.dtype)

    @pl.when(j >= n_rope_tiles)
    def _():
        o_ref[...] = y.astype(o_ref.dtype)


def _att_in(x, mods, norm_g, w_in, *, mod_row, tm, tn, col_lo, n_cols, rope):
    bsz, seq_len, _ = x.shape
    j0 = col_lo // tn
    if rope:
        tabs = _rope_tables(seq_len)
        n_q, n_rope = ATT_WIDTH // tn, 2 * ATT_WIDTH // tn
    else:
        tabs = (jnp.zeros((seq_len, LANES), F32),) * 3
        n_q = n_rope = 0
    kern = functools.partial(_att_in_kernel, tm=tm, tn=tn, n_q_tiles=n_q, n_rope_tiles=n_rope)
    tab_spec = pl.BlockSpec((tm, LANES), lambda b, i, j: (i, 0))
    return pl.pallas_call(
        kern,
        grid=(bsz, seq_len // tm, n_cols // tn),
        in_specs=_x_specs(tm, seq_len, mod_row) + [
            pl.BlockSpec((D_MODEL, tn), lambda b, i, j: (0, j0 + j)),
            tab_spec, tab_spec, tab_spec,
        ],
        out_specs=pl.BlockSpec((None, tm, tn), lambda b, i, j: (b, i, j)),
        out_shape=jax.ShapeDtypeStruct((bsz, seq_len, n_cols), BF16),
        scratch_shapes=[pltpu.VMEM((tm, D_MODEL), BF16)],
        compiler_params=_params("parallel", "parallel", "arbitrary"),
        name="att_in",
    )(x, x, x, mods, mods, norm_g, w_in, *tabs)


def _flash_kernel(lq1_ref, lk1_ref, lq2_ref, lk2_ref, sg_ref, q_ref, k_ref, v_ref, kc_ref, vc_ref,
                  z_ref, y_ref, m_scr, acc_scr, s_scr, p_scr, a_scr, sc_scr, pc_scr,
                  *, tq, tk, seq_len, lambda_init):
    q = q_ref[...]
    lane = lax.broadcasted_iota(jnp.int32, q.shape, 1)
    zero = jnp.zeros_like(q)
    qq = jnp.concatenate([jnp.where(lane < ATT_HEAD_DIM, q, zero),
                          jnp.where(lane >= ATT_HEAD_DIM, q, zero)], axis=0)

    m_scr[...] = jnp.full(m_scr.shape, -jnp.inf, F32)
    acc_scr[...] = jnp.zeros(acc_scr.shape, F32)
    n_chunks = seq_len // tk
    nt = (((1,), (1,)), ((), ()))

    def scores(kb):
        return lax.dot_general(qq, kb, nt, preferred_element_type=F32)

    def softmax(blocks, a_ref):
        for r in range(0, 2 * tq, FLASH_ROWS):
            rows = slice(r, r + FLASH_ROWS)
            cols = [s_ref[rows, j:j + LANES] for s_ref, _, width in blocks for j in range(0, width, LANES)]
            m_old = m_scr[rows, :]
            m_new = jnp.maximum(m_old, jnp.max(functools.reduce(jnp.maximum, cols), axis=1, keepdims=True))
            a_ref[rows, :] = jnp.exp2(m_old - m_new)
            m_scr[rows, :] = m_new
            for s_ref, p_ref, width in blocks:
                for j in range(0, width, LANES):
                    p_ref[rows, j:j + LANES] = jnp.exp2((s_ref[rows, j:j + LANES] - m_new).astype(BF16))

    def with_ones(vb):
        return jnp.concatenate([vb, jnp.ones_like(vb)], axis=1)

    def accumulate(a_ref, terms):
        pv = [jnp.dot(p_ref[...], with_ones(vb), preferred_element_type=F32) for p_ref, vb in terms]
        total, alpha = functools.reduce(jnp.add, pv), a_ref[...]
        for cs in (slice(0, HEAD_W), slice(HEAD_W, 2 * HEAD_W)):
            acc_scr[:, cs] = alpha * acc_scr[:, cs] + total[:, cs]

    def k_chunk(c):
        return k_ref[pl.ds(pl.multiple_of(c * tk, tk), tk), :]

    def v_chunk(c):
        return v_ref[pl.ds(pl.multiple_of(c * tk, tk), tk), :]

    ctx_w = kc_ref.shape[0]
    sc_scr[...] = scores(kc_ref[...])
    s_scr[0] = scores(k_chunk(0))
    softmax([(s_scr.at[0], p_scr.at[0], tk)], a_scr.at[0])
    s_scr[1] = scores(k_chunk(1))

    def stage(c, cur, prv, last):
        if not last:
            s_scr[prv] = scores(k_chunk(c + 1))
        accumulate(a_scr.at[prv], [(p_scr.at[prv], v_chunk(c - 1))])
        blocks = [(s_scr.at[cur], p_scr.at[cur], tk)]
        if last:
            blocks.append((sc_scr, pc_scr, ctx_w))
        softmax(blocks, a_scr.at[cur])

    def body(i, carry):
        stage(2 * i + 1, 1, 0, False)
        stage(2 * i + 2, 0, 1, False)
        return carry

    lax.fori_loop(0, n_chunks // 2 - 1, body, 0)
    stage(n_chunks - 1, 1, 0, True)
    accumulate(a_scr.at[1], [(p_scr.at[1], v_chunk(n_chunks - 1)), (pc_scr, vc_ref[...])])

    lam = (jnp.exp(jnp.sum(lq1_ref[...] * lk1_ref[...], axis=1, keepdims=True))
           - jnp.exp(jnp.sum(lq2_ref[...] * lk2_ref[...], axis=1, keepdims=True)) + lambda_init)
    o1 = acc_scr[0:tq, 0:HEAD_W] / acc_scr[0:tq, HEAD_W:]
    o2 = acc_scr[tq:, 0:HEAD_W] / acc_scr[tq:, HEAD_W:]
    o = o1 - lam * o2
    o = o * lax.rsqrt(jnp.mean(o * o, axis=-1, keepdims=True) + SUBLN_EPS)
    o = (o * sg_ref[...]) * (1.0 - lambda_init)
    y_ref[...] = (o * _silu(z_ref[...].astype(F32))).astype(y_ref.dtype)


def _flash(qkvz, kvc, lam_params, subln_g, *, lambda_init, tq, tk):
    bsz, seq_len, _ = qkvz.shape
    ctx_len = kvc.shape[1]
    nh = N_ATT_HEADS
    assert seq_len % (2 * tk) == 0 and ctx_len % LANES == 0
    kern = functools.partial(_flash_kernel, tq=tq, tk=tk, seq_len=seq_len, lambda_init=lambda_init)
    vec = pl.BlockSpec((1, ATT_HEAD_DIM), lambda b, h, i: (0, 0))
    return pl.pallas_call(
        kern,
        grid=(bsz, nh, seq_len // tq),
        in_specs=[vec, vec, vec, vec,
                  pl.BlockSpec((1, HEAD_W), lambda b, h, i: (0, 0)),
                  pl.BlockSpec((None, tq, HEAD_W), lambda b, h, i: (b, i, h)),
                  pl.BlockSpec((None, seq_len, HEAD_W), lambda b, h, i: (b, 0, nh + h)),
                  pl.BlockSpec((None, seq_len, HEAD_W), lambda b, h, i: (b, 0, 2 * nh + h)),
                  pl.BlockSpec((None, ctx_len, HEAD_W), lambda b, h, i: (b, 0, h)),
                  pl.BlockSpec((None, ctx_len, HEAD_W), lambda b, h, i: (b, 0, nh + h)),
                  pl.BlockSpec((None, tq, HEAD_W), lambda b, h, i: (b, i, 3 * nh + h))],
        out_specs=pl.BlockSpec((None, tq, HEAD_W), lambda b, h, i: (b, i, h)),
        out_shape=jax.ShapeDtypeStruct((bsz, seq_len, ATT_WIDTH), BF16),
        scratch_shapes=[pltpu.VMEM((2 * tq, LANES), F32), pltpu.VMEM((2 * tq, 2 * HEAD_W), F32),
                        pltpu.VMEM((2, 2 * tq, tk), F32), pltpu.VMEM((2, 2 * tq, tk), BF16),
                        pltpu.VMEM((2, 2 * tq, LANES), F32),
                        pltpu.VMEM((2 * tq, ctx_len), F32), pltpu.VMEM((2 * tq, ctx_len), BF16)],
        compiler_params=_params("parallel", "parallel", "arbitrary"),
        name="diff_flash",
    )(*lam_params, subln_g, qkvz, qkvz, qkvz, kvc, kvc, qkvz)


def _final_norm_kernel(x_ref, g_ref, o_ref):
    xv = x_ref[...]
    o_ref[...] = xv * lax.rsqrt(jnp.mean(xv * xv, axis=-1, keepdims=True) + EPS) * g_ref[...]


def _final_norm(x, g, *, tm):
    bsz, seq_len, _ = x.shape
    return pl.pallas_call(
        _final_norm_kernel,
        grid=(bsz, seq_len // tm),
        in_specs=[pl.BlockSpec((None, tm, D_MODEL), lambda b, i: (b, i, 0)),
                  pl.BlockSpec((1, D_MODEL), lambda b, i: (0, 0))],
        out_specs=pl.BlockSpec((None, tm, D_MODEL), lambda b, i: (b, i, 0)),
        out_shape=jax.ShapeDtypeStruct(x.shape, F32),
        compiler_params=_params("parallel", "parallel"),
        name="final_norm",
    )(x, g)


def kernel(x, c, ctx, c_ctx, norm_g, w_mod, b_mod, final_g, conv_w_in, conv_dw_w, conv_dw_b,
           conv_ln_g, conv_ln_b, conv_w_out, pool_w_in, pool_w_grp, pool_scale, pool_w_out,
           att_w_in, att_lam_q1, att_lam_k1, att_lam_q2, att_lam_k2, att_subln_g, att_w_out):
    bsz = x.shape[0]
    ctx_len = ctx.shape[1]
    n_att = DEPTH // N_MIXERS
    last_ctx_reader = N_MIXERS * (n_att - 1) + ATT_ID if n_att > 0 else -1
    ctx_row = bsz

    cc = jnp.zeros((MOD_ROWS, D_MODEL), F32).at[:bsz].set(c).at[ctx_row].set(c_ctx)
    mods_all = _modulation(cc, w_mod, b_mod).reshape(DEPTH, MOD_ROWS, 3, 1, D_MODEL)

    tm, tn = 512, 512
    fin_g = final_g.reshape(1, D_MODEL)
    final_fused = (DEPTH - 1) % N_MIXERS == CONV_ID
    xs, cs = x, ctx
    for i in range(DEPTH):
        kind, j = i % N_MIXERS, i // N_MIXERS
        ctx_live = i <= last_ctx_reader
        ctx_update = i < last_ctx_reader
        mods = mods_all[i]
        g = norm_g[i].reshape(1, D_MODEL)
        last_layer = i == DEPTH - 1
        streams = [(xs, None, tm)]
        if ctx_update:
            streams.append((cs, ctx_row, ctx_len))
        new = []
        if kind == CONV_ID:
            w_in = conv_w_in[j].astype(BF16)
            w_out = conv_w_out[j].astype(BF16)
            for s, row, tms in streams:
                v, z = _conv_in(s, mods, g, w_in, conv_dw_w[j], conv_dw_b[j].reshape(1, E_WIDE),
                                mod_row=row, tm=tms, tn=tn)
                new.append(_conv_out(v, z, conv_ln_g[j].reshape(1, E_WIDE),
                                     conv_ln_b[j].reshape(1, E_WIDE), w_out, s, mods, fin_g,
                                     mod_row=row, tm=min(tms, 256),
                                     final_norm=last_layer and row is None))
        elif kind == POOL_ID:
            w_in = pool_w_in[j].astype(BF16)
            w_grp = pool_w_grp[j].astype(BF16)
            w_out = pool_w_out[j].astype(BF16)
            for s, row, tms in streams:
                act = _pool_in(s, mods, g, w_in, w_grp, pool_scale[j].reshape(1, E_WIDE),
                               mod_row=row, tm=tms)
                new.append(_proj_out(act, w_out, s, mods, mod_row=row, tm=tms))
        else:
            lambda_init = 0.8 - 0.6 * math.exp(-0.3 * i)
            w_in = att_w_in[j].astype(BF16)
            w_out = att_w_out[j].astype(BF16)
            qkvz = _att_in(xs, mods, g, w_in, mod_row=None, tm=tm, tn=tn,
                           col_lo=0, n_cols=4 * ATT_WIDTH, rope=True)
            kvc = _att_in(cs, mods, g, w_in, mod_row=ctx_row, tm=ctx_len, tn=tn,
                          col_lo=ATT_WIDTH, n_cols=2 * ATT_WIDTH, rope=False)
            lam_params = [p[j].reshape(1, ATT_HEAD_DIM)
                          for p in (att_lam_q1, att_lam_k1, att_lam_q2, att_lam_k2)]
            y = _flash(qkvz, kvc, lam_params, att_subln_g[j].reshape(1, HEAD_W),
                       lambda_init=lambda_init, tq=512, tk=512)
            new.append(_proj_out(y, w_out, xs, mods, mod_row=None, tm=tm))
            if ctx_update:
                raise NotImplementedError("context-side queries are not needed at this depth")
        xs = new[0]
        if ctx_update:
            cs = new[1]
        del ctx_live
    return xs if final_fused else _final_norm(xs, fin_g, tm=tm)
```

```python
import functools
import math

import jax
import jax.numpy as jnp
from jax import lax
from jax.experimental import pallas as pl
from jax.experimental.pallas import tpu as pltpu

D_MODEL = 2048
DEPTH = 4
GRID_W = 64
N_MIXERS = 3
CONV_ID, POOL_ID, ATT_ID = 0, 1, 2
E_WIDE = 2 * D_MODEL
CONV_K = 31
CONV_HALF = CONV_K // 2
POOL_WINDOWS = (2, 4, 8, 16)
POOL_GROUP = E_WIDE // len(POOL_WINDOWS)
ATT_WIDTH = D_MODEL
ATT_HEAD_DIM = 64
HEAD_W = 2 * ATT_HEAD_DIM
N_ATT_HEADS = ATT_WIDTH // HEAD_W
ROPE_BASE = 10000.0
EPS = 1e-6
SUBLN_EPS = 1e-5

LANES = 128
HALO = 16
MOD_ROWS = 8
VMEM_LIMIT = 56 * 1024 * 1024

F32 = jnp.float32
BF16 = jnp.bfloat16


def _silu(v):
    return v * jax.nn.sigmoid(v)


def _params(*sem, flags=None):
    return pltpu.CompilerParams(dimension_semantics=sem, vmem_limit_bytes=VMEM_LIMIT, flags=flags)


def _norm_mod(xv, g, sc, sh):
    xv = xv.astype(F32)
    y = xv * lax.rsqrt(jnp.mean(xv * xv, axis=-1, keepdims=True) + EPS)
    return ((y * g) * (1.0 + sc) + sh).astype(BF16)


def _mod_kernel(c_ref, w_ref, b_ref, o_ref):
    s = _silu(c_ref[...])
    o_ref[...] = jnp.dot(s, w_ref[...], precision=lax.Precision.HIGHEST,
                         preferred_element_type=F32) + b_ref[...]


def _modulation(cc, w_mod, b_mod):
    tn = 1024
    n = 3 * D_MODEL
    return pl.pallas_call(
        _mod_kernel,
        grid=(DEPTH, n // tn),
        in_specs=[
            pl.BlockSpec((MOD_ROWS, D_MODEL), lambda l, j: (0, 0)),
            pl.BlockSpec((None, D_MODEL, tn), lambda l, j: (l, 0, j)),
            pl.BlockSpec((None, 1, tn), lambda l, j: (l, 0, j)),
        ],
        out_specs=pl.BlockSpec((None, MOD_ROWS, tn), lambda l, j: (l, 0, j)),
        out_shape=jax.ShapeDtypeStruct((DEPTH, MOD_ROWS, n), F32),
        compiler_params=_params("parallel", "parallel"),
        name="modulation",
    )(cc, w_mod, b_mod.reshape(DEPTH, 1, n))


def _x_specs(tm, seq_len, mod_row):
    nh = tm // HALO
    last = seq_len // HALO - 1
    row = (lambda b: b) if mod_row is None else (lambda b: mod_row)
    return [
        pl.BlockSpec((None, HALO, D_MODEL), lambda b, i, j: (b, jnp.maximum(i * nh - 1, 0), 0)),
        pl.BlockSpec((None, tm, D_MODEL), lambda b, i, j: (b, i, 0)),
        pl.BlockSpec((None, HALO, D_MODEL), lambda b, i, j: (b, jnp.minimum((i + 1) * nh, last), 0)),
        pl.BlockSpec((None, None, 1, D_MODEL), lambda b, i, j: (row(b), 0, 0, 0)),
        pl.BlockSpec((None, None, 1, D_MODEL), lambda b, i, j: (row(b), 1, 0, 0)),
        pl.BlockSpec((1, D_MODEL), lambda b, i, j: (0, 0)),
    ]


def _fill_h(h_scr, xp_ref, x_ref, xn_ref, sh_ref, sc_ref, g_ref, tm):
    g, sc, sh = g_ref[...], sc_ref[...], sh_ref[...]
    h_scr[0:HALO, :] = _norm_mod(xp_ref[...], g, sc, sh)
    h_scr[HALO:HALO + tm, :] = _norm_mod(x_ref[...], g, sc, sh)
    h_scr[HALO + tm:, :] = _norm_mod(xn_ref[...], g, sc, sh)


def _seq_row_mask(i, tm, seq_len, tn):
    row = lax.broadcasted_iota(jnp.int32, (tm + 2 * HALO, tn), 0) + (i * tm - HALO)
    return (row >= 0) & (row < seq_len)


SUB = 8
CONV_GROUPS = 4
SHIFT_ROWS = 64


def _conv_in_kernel(xp_ref, x_ref, xn_ref, sh_ref, sc_ref, g_ref, wa_ref, wb_ref, wz_ref,
                    dw_ref, db_ref, v_ref, z_ref, h_scr, u_scr, us_scr, wt_scr, *, tm, tn, seq_len):
    i, j = pl.program_id(1), pl.program_id(2)
    ext = tm + 2 * HALO
    n_grp = ext // SUB
    hw = tn // 2

    @pl.when(j == 0)
    def _():
        _fill_h(h_scr, xp_ref, x_ref, xn_ref, sh_ref, sc_ref, g_ref, tm)

    for k in range(CONV_K):
        wt_scr[k] = jnp.broadcast_to(dw_ref[k:k + 1, :], (SUB, tn))
    bias = jnp.broadcast_to(db_ref[...], (SUB, tn))
    mask = _seq_row_mask(i, tm, seq_len, hw)
    h = h_scr[...]
    halves = (slice(0, hw), slice(hw, tn))

    def glu(cs):
        a = jnp.dot(h, wa_ref[:, cs], preferred_element_type=F32)
        b = jnp.dot(h, wb_ref[:, cs], preferred_element_type=F32)
        u = jnp.where(mask, a * jax.nn.sigmoid(b), 0.0)
        u_scr[:, cs] = u
        us_scr[0, :, :, cs] = u.reshape(n_grp, SUB, hw)

    def conv(cs):
        for r in range(1, SUB):
            for c0 in range(0, ext - SUB, SHIFT_ROWS):
                n = min(SHIFT_ROWS, ext - SUB - c0)
                us_scr[r, c0 // SUB:(c0 + n) // SUB, :, cs] = (
                    u_scr[pl.ds(c0 + r, n), cs].reshape(n // SUB, SUB, hw))
        for g0 in range(0, tm // SUB, CONV_GROUPS):
            acc = jnp.broadcast_to(bias[:, cs], (CONV_GROUPS, SUB, hw))
            for k in range(CONV_K):
                off = HALO - CONV_HALF + k
                lo = g0 + off // SUB
                acc = acc + us_scr[off % SUB, lo:lo + CONV_GROUPS, :, cs] * wt_scr[k, :, cs]
            v_ref[g0 * SUB:(g0 + CONV_GROUPS) * SUB, cs] = (
                acc.reshape(CONV_GROUPS * SUB, hw).astype(v_ref.dtype))

    glu(halves[0])
    glu(halves[1])
    conv(halves[0])
    z_ref[...] = jnp.dot(h_scr[HALO:HALO + tm, :], wz_ref[...],
                         preferred_element_type=F32).astype(z_ref.dtype)
    conv(halves[1])


def _conv_in(x, mods, norm_g, w_in, dw_w, dw_b, *, mod_row, tm, tn):
    bsz, seq_len, _ = x.shape
    nj = E_WIDE // tn
    kern = functools.partial(_conv_in_kernel, tm=tm, tn=tn, seq_len=seq_len)
    out = jax.ShapeDtypeStruct((bsz, seq_len, E_WIDE), BF16)
    return pl.pallas_call(
        kern,
        grid=(bsz, seq_len // tm, nj),
        in_specs=_x_specs(tm, seq_len, mod_row) + [
            pl.BlockSpec((D_MODEL, tn), lambda b, i, j: (0, j)),
            pl.BlockSpec((D_MODEL, tn), lambda b, i, j: (0, nj + j)),
            pl.BlockSpec((D_MODEL, tn), lambda b, i, j: (0, 2 * nj + j)),
            pl.BlockSpec((CONV_K, tn), lambda b, i, j: (0, j)),
            pl.BlockSpec((1, tn), lambda b, i, j: (0, j)),
        ],
        out_specs=[pl.BlockSpec((None, tm, tn), lambda b, i, j: (b, i, j))] * 2,
        out_shape=[out, out],
        scratch_shapes=[pltpu.VMEM((tm + 2 * HALO, D_MODEL), BF16),
                        pltpu.VMEM((tm + 2 * HALO, tn), F32),
                        pltpu.VMEM((SUB, (tm + 2 * HALO) // SUB, SUB, tn), F32),
                        pltpu.VMEM((CONV_K, SUB, tn), F32)],
        compiler_params=_params("parallel", "parallel", "arbitrary"),
        name="conv_in",
    )(x, x, x, mods, mods, norm_g, w_in, w_in, w_in, dw_w, dw_b)


LN_ROWS = 16
OUT_KC = 1024


def _rmsnorm_rows(xv, g):
    return xv * lax.rsqrt(jnp.mean(xv * xv, axis=-1, keepdims=True) + EPS) * g


def _conv_out_kernel(v_ref, z_ref, lg_ref, lb_ref, w_ref, x_ref, gt_ref, fg_ref, o_ref, *, tm, final_norm):
    mus, rss = [], []
    for r in range(0, tm, LN_ROWS):
        v = v_ref[r:r + LN_ROWS, :].astype(F32)
        mu = jnp.mean(v, axis=-1, keepdims=True)
        d = v - mu
        mus.append(mu)
        rss.append(lax.rsqrt(jnp.mean(d * d, axis=-1, keepdims=True) + EPS))
    mu, rs = jnp.concatenate(mus, axis=0), jnp.concatenate(rss, axis=0)

    acc = None
    for k0 in range(0, E_WIDE, OUT_KC):
        ks = slice(k0, k0 + OUT_KC)
        y = _silu((v_ref[:, ks].astype(F32) - mu) * rs * lg_ref[:, ks] + lb_ref[:, ks])
        act = (y * _silu(z_ref[:, ks].astype(F32))).astype(BF16)
        part = jnp.dot(act, w_ref[ks, :], preferred_element_type=F32)
        acc = part if acc is None else acc + part
    o = x_ref[...] + gt_ref[...] * acc
    o_ref[...] = _rmsnorm_rows(o, fg_ref[...]) if final_norm else o


def _resident(shape):
    return pl.BlockSpec(shape, lambda *_: (0,) * len(shape), pipeline_mode=pl.Buffered(1))


def _conv_out(v, z, ln_g, ln_b, w_out, x, mods, final_g, *, mod_row, tm, final_norm):
    bsz, seq_len, _ = x.shape
    row = (lambda b: b) if mod_row is None else (lambda b: mod_row)
    kern = functools.partial(_conv_out_kernel, tm=tm, final_norm=final_norm)
    return pl.pallas_call(
        kern,
        grid=(bsz, seq_len // tm),
        in_specs=[
            pl.BlockSpec((None, tm, E_WIDE), lambda b, i: (b, i, 0)),
            pl.BlockSpec((None, tm, E_WIDE), lambda b, i: (b, i, 0)),
            _resident((1, E_WIDE)),
            _resident((1, E_WIDE)),
            _resident((E_WIDE, D_MODEL)),
            pl.BlockSpec((None, tm, D_MODEL), lambda b, i: (b, i, 0)),
            pl.BlockSpec((None, None, 1, D_MODEL), lambda b, i: (row(b), 2, 0, 0)),
            _resident((1, D_MODEL)),
        ],
        out_specs=pl.BlockSpec((None, tm, D_MODEL), lambda b, i: (b, i, 0)),
        out_shape=jax.ShapeDtypeStruct(x.shape, F32),
        compiler_params=_params("parallel", "parallel"),
        name="conv_out",
    )(v, z, ln_g, ln_b, w_out, x, mods, final_g)


POOL_ROWS = 64


def _pool_in_kernel(xp_ref, x_ref, xn_ref, sh_ref, sc_ref, g_ref, wu_ref, wz_ref, wg_ref,
                    ps_ref, o_ref, h_scr, u_scr, p_scr, *, tm, seq_len):
    i, grp = pl.program_id(1), pl.program_id(2)
    tn = POOL_GROUP

    @pl.when(grp == 0)
    def _():
        _fill_h(h_scr, xp_ref, x_ref, xn_ref, sh_ref, sc_ref, g_ref, tm)

    u = jnp.dot(h_scr[...], wu_ref[...], preferred_element_type=F32)
    u_scr[...] = jnp.where(_seq_row_mask(i, tm, seq_len, tn), u, 0.0)

    for gi, win in enumerate(POOL_WINDOWS):
        @pl.when(grp == gi)
        def _(win=win):
            half = win // 2

            for r0 in range(0, tm, POOL_ROWS):
                t = lax.broadcasted_iota(jnp.int32, (POOL_ROWS, 1), 0) + (i * tm + r0)
                cnt = jnp.minimum(t + half, seq_len) - jnp.maximum(t - half, 0)
                acc = u_scr[pl.ds(r0 + (HALO - half), POOL_ROWS), :]
                for d in range(1, win):
                    acc = acc + u_scr[pl.ds(r0 + (HALO - half + d), POOL_ROWS), :]
                p = acc / cnt.astype(F32) - u_scr[pl.ds(r0 + HALO, POOL_ROWS), :]
                p_scr[pl.ds(r0, POOL_ROWS), :] = p.astype(BF16)

    z = jnp.dot(h_scr[HALO:HALO + tm, :], wz_ref[...], preferred_element_type=F32)
    q = jnp.dot(p_scr[...], wg_ref[...], preferred_element_type=F32)
    o_ref[...] = (q * ps_ref[...] * _silu(z)).astype(o_ref.dtype)


def _pool_in(x, mods, norm_g, w_in, w_grp, scale, *, mod_row, tm):
    bsz, seq_len, _ = x.shape
    ng = len(POOL_WINDOWS)
    tn = POOL_GROUP
    kern = functools.partial(_pool_in_kernel, tm=tm, seq_len=seq_len)
    return pl.pallas_call(
        kern,
        grid=(bsz, seq_len // tm, ng),
        in_specs=_x_specs(tm, seq_len, mod_row) + [
            pl.BlockSpec((D_MODEL, tn), lambda b, i, j: (0, j)),
            pl.BlockSpec((D_MODEL, tn), lambda b, i, j: (0, ng + j)),
            pl.BlockSpec((None, tn, tn), lambda b, i, j: (j, 0, 0)),
            pl.BlockSpec((1, tn), lambda b, i, j: (0, j)),
        ],
        out_specs=pl.BlockSpec((None, tm, tn), lambda b, i, j: (b, i, j)),
        out_shape=jax.ShapeDtypeStruct((bsz, seq_len, E_WIDE), BF16),
        scratch_shapes=[pltpu.VMEM((tm + 2 * HALO, D_MODEL), BF16),
                        pltpu.VMEM((tm + 2 * HALO, tn), F32),
                        pltpu.VMEM((tm, tn), BF16)],
        compiler_params=_params("parallel", "parallel", "arbitrary"),
        name="pool_in",
    )(x, x, x, mods, mods, norm_g, w_in, w_in, w_grp, scale)


def _proj_out_kernel(a_ref, w_ref, x_ref, gt_ref, o_ref):
    y = jnp.dot(a_ref[...], w_ref[...], preferred_element_type=F32)
    o_ref[...] = x_ref[...] + gt_ref[...] * y


def _proj_out(act, w_out, x, mods, *, mod_row, tm):
    bsz, seq_len, _ = x.shape
    kdim = act.shape[-1]
    row = (lambda b: b) if mod_row is None else (lambda b: mod_row)
    return pl.pallas_call(
        _proj_out_kernel,
        grid=(bsz, seq_len // tm),
        in_specs=[
            pl.BlockSpec((None, tm, kdim), lambda b, i: (b, i, 0)),
            _resident((kdim, D_MODEL)),
            pl.BlockSpec((None, tm, D_MODEL), lambda b, i: (b, i, 0)),
            pl.BlockSpec((None, None, 1, D_MODEL), lambda b, i: (row(b), 2, 0, 0)),
        ],
        out_specs=pl.BlockSpec((None, tm, D_MODEL), lambda b, i: (b, i, 0)),
        out_shape=jax.ShapeDtypeStruct(x.shape, F32),
        compiler_params=_params("parallel", "parallel"),
        name="proj_out",
    )(act, w_out, x, mods)


QUERY_SCALE = ATT_HEAD_DIM ** -0.5 * math.log2(math.e)
FLASH_ROWS = 32


def _rope_tables(seq_len):
    t = jnp.arange(seq_len)
    n_freq = ATT_HEAD_DIM // 4
    freqs = ROPE_BASE ** (-jnp.arange(n_freq, dtype=F32) / n_freq)
    pos = jnp.stack([t // GRID_W, t % GRID_W], axis=1).astype(F32)
    ang = pos[:, :, None] * freqs
    cos, sin = jnp.cos(ang), jnp.sin(ang)
    zero = jnp.zeros_like(sin)

    def table(first, second):
        tab = jnp.stack([first, second], axis=2)
        return jnp.tile(tab.reshape(seq_len, ATT_HEAD_DIM), (1, 2))

    return table(cos, cos), table(-sin, zero), table(zero, sin)


def _att_in_kernel(xp_ref, x_ref, xn_ref, sh_ref, sc_ref, g_ref, w_ref, cos_ref, sa_ref, sb_ref,
                   o_ref, h_scr, *, tm, tn, n_q_tiles, n_rope_tiles):
    del xp_ref, xn_ref
    j = pl.program_id(2)

    @pl.when(j == 0)
    def _():
        h_scr[...] = _norm_mod(x_ref[...], g_ref[...], sc_ref[...], sh_ref[...])

    @pl.when(j < n_rope_tiles)
    def _():
        y = jnp.dot(h_scr[...], w_ref[...], preferred_element_type=F32)
        scale = jnp.where(j < n_q_tiles, QUERY_SCALE, 1.0)
        cos, sa, sb = cos_ref[...] * scale, sa_ref[...] * scale, sb_ref[...] * scale
        half = ATT_HEAD_DIM // 4
        for c in range(tn // LANES):
            yc = y[:, c * LANES:(c + 1) * LANES]
            r = (yc * cos + pltpu.roll(yc, LANES - half, axis=1) * sa
                 + pltpu.roll(yc, half, axis=1) * sb)
            o_ref[:, c * LANES:(c + 1) * LANES] = r.astype(o_ref.dtype)

    @pl.when(j >= n_rope_tiles)
    def _():
        o_ref[...] = jnp.dot(h_scr[...], w_ref[...], preferred_element_type=F32).astype(o_ref.dtype)


def _att_in(x, mods, norm_g, w_in, *, mod_row, tm, tn, col_lo, n_cols, rope):
    bsz, seq_len, _ = x.shape
    j0 = col_lo // tn
    if rope:
        tabs = _rope_tables(seq_len)
        n_q, n_rope = ATT_WIDTH // tn, 2 * ATT_WIDTH // tn
    else:
        tabs = (jnp.zeros((seq_len, LANES), F32),) * 3
        n_q = n_rope = 0
    kern = functools.partial(_att_in_kernel, tm=tm, tn=tn, n_q_tiles=n_q, n_rope_tiles=n_rope)
    tab_spec = pl.BlockSpec((tm, LANES), lambda b, i, j: (i, 0))
    return pl.pallas_call(
        kern,
        grid=(bsz, seq_len // tm, n_cols // tn),
        in_specs=_x_specs(tm, seq_len, mod_row) + [
            pl.BlockSpec((D_MODEL, tn), lambda b, i, j: (0, j0 + j)),
            tab_spec, tab_spec, tab_spec,
        ],
        out_specs=pl.BlockSpec((None, tm, tn), lambda b, i, j: (b, i, j)),
        out_shape=jax.ShapeDtypeStruct((bsz, seq_len, n_cols), BF16),
        scratch_shapes=[pltpu.VMEM((tm, D_MODEL), BF16)],
        compiler_params=_params("parallel", "parallel", "arbitrary"),
        name="att_in",
    )(x, x, x, mods, mods, norm_g, w_in, *tabs)


def _flash_kernel(lq1_ref, lk1_ref, lq2_ref, lk2_ref, sg_ref, q_ref, k_ref, v_ref, kc_ref, vc_ref,
                  z_ref, y_ref, m_scr, acc_scr, s_scr, p_scr, a_scr, sc_scr, pc_scr,
                  *, tq, tk, seq_len, lambda_init):
    q = q_ref[...]
    lane = lax.broadcasted_iota(jnp.int32, q.shape, 1)
    zero = jnp.zeros_like(q)
    qq = jnp.concatenate([jnp.where(lane < ATT_HEAD_DIM, q, zero),
                          jnp.where(lane >= ATT_HEAD_DIM, q, zero)], axis=0)

    m_scr[...] = jnp.full(m_scr.shape, -jnp.inf, F32)
    acc_scr[...] = jnp.zeros(acc_scr.shape, F32)
    n_chunks = seq_len // tk
    nt = (((1,), (1,)), ((), ()))

    def scores(kb):
        return lax.dot_general(qq, kb, nt, preferred_element_type=F32)

    def softmax(blocks, a_ref):
        for r in range(0, 2 * tq, FLASH_ROWS):
            rows = slice(r, r + FLASH_ROWS)
            cols = [s_ref[rows, j:j + LANES] for s_ref, _, width in blocks for j in range(0, width, LANES)]
            m_old = m_scr[rows, :]
            m_new = jnp.maximum(m_old, jnp.max(functools.reduce(jnp.maximum, cols), axis=1, keepdims=True))
            a_ref[rows, :] = jnp.exp2(m_old - m_new)
            m_scr[rows, :] = m_new
            for s_ref, p_ref, width in blocks:
                for j in range(0, width, LANES):
                    p_ref[rows, j:j + LANES] = jnp.exp2((s_ref[rows, j:j + LANES] - m_new).astype(BF16))

    def with_ones(vb):
        return jnp.concatenate([vb, jnp.ones_like(vb)], axis=1)

    def accumulate(a_ref, terms):
        pv = [jnp.dot(p_ref[...], with_ones(vb), preferred_element_type=F32) for p_ref, vb in terms]
        total, alpha = functools.reduce(jnp.add, pv), a_ref[...]
        for cs in (slice(0, HEAD_W), slice(HEAD_W, 2 * HEAD_W)):
            acc_scr[:, cs] = alpha * acc_scr[:, cs] + total[:, cs]

    def k_chunk(c):
        return k_ref[pl.ds(pl.multiple_of(c * tk, tk), tk), :]

    def v_chunk(c):
        return v_ref[pl.ds(pl.multiple_of(c * tk, tk), tk), :]

    ctx_w = kc_ref.shape[0]
    sc_scr[...] = scores(kc_ref[...])
    s_scr[0] = scores(k_chunk(0))
    softmax([(s_scr.at[0], p_scr.at[0], tk)], a_scr.at[0])
    s_scr[1] = scores(k_chunk(1))

    def stage(c, cur, prv, last):
        if not last:
            s_scr[prv] = scores(k_chunk(c + 1))
        accumulate(a_scr.at[prv], [(p_scr.at[prv], v_chunk(c - 1))])
        blocks = [(s_scr.at[cur], p_scr.at[cur], tk)]
        if last:
            blocks.append((sc_scr, pc_scr, ctx_w))
        softmax(blocks, a_scr.at[cur])

    def body(i, carry):
        stage(2 * i + 1, 1, 0, False)
        stage(2 * i + 2, 0, 1, False)
        return carry

    lax.fori_loop(0, n_chunks // 2 - 1, body, 0)
    stage(n_chunks - 1, 1, 0, True)
    accumulate(a_scr.at[1], [(p_scr.at[1], v_chunk(n_chunks - 1)), (pc_scr, vc_ref[...])])

    lam = (jnp.exp(jnp.sum(lq1_ref[...] * lk1_ref[...], axis=1, keepdims=True))
           - jnp.exp(jnp.sum(lq2_ref[...] * lk2_ref[...], axis=1, keepdims=True)) + lambda_init)
    o1 = acc_scr[0:tq, 0:HEAD_W] / acc_scr[0:tq, HEAD_W:]
    o2 = acc_scr[tq:, 0:HEAD_W] / acc_scr[tq:, HEAD_W:]
    o = o1 - lam * o2
    o = o * lax.rsqrt(jnp.mean(o * o, axis=-1, keepdims=True) + SUBLN_EPS)
    o = (o * sg_ref[...]) * (1.0 - lambda_init)
    y_ref[...] = (o * _silu(z_ref[...].astype(F32))).astype(y_ref.dtype)


def _flash(qkvz, kvc, lam_params, subln_g, *, lambda_init, tq, tk):
    bsz, seq_len, _ = qkvz.shape
    ctx_len = kvc.shape[1]
    nh = N_ATT_HEADS
    assert seq_len % (2 * tk) == 0 and ctx_len % LANES == 0
    kern = functools.partial(_flash_kernel, tq=tq, tk=tk, seq_len=seq_len, lambda_init=lambda_init)
    vec = pl.BlockSpec((1, ATT_HEAD_DIM), lambda b, h, i: (0, 0))
    return pl.pallas_call(
        kern,
        grid=(bsz, nh, seq_len // tq),
        in_specs=[vec, vec, vec, vec,
                  pl.BlockSpec((1, HEAD_W), lambda b, h, i: (0, 0)),
                  pl.BlockSpec((None, tq, HEAD_W), lambda b, h, i: (b, i, h)),
                  pl.BlockSpec((None, seq_len, HEAD_W), lambda b, h, i: (b, 0, nh + h)),
                  pl.BlockSpec((None, seq_len, HEAD_W), lambda b, h, i: (b, 0, 2 * nh + h)),
                  pl.BlockSpec((None, ctx_len, HEAD_W), lambda b, h, i: (b, 0, h)),
                  pl.BlockSpec((None, ctx_len, HEAD_W), lambda b, h, i: (b, 0, nh + h)),
                  pl.BlockSpec((None, tq, HEAD_W), lambda b, h, i: (b, i, 3 * nh + h))],
        out_specs=pl.BlockSpec((None, tq, HEAD_W), lambda b, h, i: (b, i, h)),
        out_shape=jax.ShapeDtypeStruct((bsz, seq_len, ATT_WIDTH), BF16),
        scratch_shapes=[pltpu.VMEM((2 * tq, LANES), F32), pltpu.VMEM((2 * tq, 2 * HEAD_W), F32),
                        pltpu.VMEM((2, 2 * tq, tk), F32), pltpu.VMEM((2, 2 * tq, tk), BF16),
                        pltpu.VMEM((2, 2 * tq, LANES), F32),
                        pltpu.VMEM((2 * tq, ctx_len), F32), pltpu.VMEM((2 * tq, ctx_len), BF16)],
        compiler_params=_params("parallel", "parallel", "arbitrary"),
        name="diff_flash",
    )(*lam_params, subln_g, qkvz, qkvz, qkvz, kvc, kvc, qkvz)


def _final_norm_kernel(x_ref, g_ref, o_ref):
    xv = x_ref[...]
    o_ref[...] = xv * lax.rsqrt(jnp.mean(xv * xv, axis=-1, keepdims=True) + EPS) * g_ref[...]


def _final_norm(x, g, *, tm):
    bsz, seq_len, _ = x.shape
    return pl.pallas_call(
        _final_norm_kernel,
        grid=(bsz, seq_len // tm),
        in_specs=[pl.BlockSpec((None, tm, D_MODEL), lambda b, i: (b, i, 0)),
                  pl.BlockSpec((1, D_MODEL), lambda b, i: (0, 0))],
        out_specs=pl.BlockSpec((None, tm, D_MODEL), lambda b, i: (b, i, 0)),
        out_shape=jax.ShapeDtypeStruct(x.shape, F32),
        compiler_params=_params("parallel", "parallel"),
        name="final_norm",
    )(x, g)


def kernel(x, c, ctx, c_ctx, norm_g, w_mod, b_mod, final_g, conv_w_in, conv_dw_w, conv_dw_b,
           conv_ln_g, conv_ln_b, conv_w_out, pool_w_in, pool_w_grp, pool_scale, pool_w_out,
           att_w_in, att_lam_q1, att_lam_k1, att_lam_q2, att_lam_k2, att_subln_g, att_w_out):
    bsz = x.shape[0]
    ctx_len = ctx.shape[1]
    n_att = DEPTH // N_MIXERS
    last_ctx_reader = N_MIXERS * (n_att - 1) + ATT_ID if n_att > 0 else -1
    ctx_row = bsz

    cc = jnp.zeros((MOD_ROWS, D_MODEL), F32).at[:bsz].set(c).at[ctx_row].set(c_ctx)
    mods_all = _modulation(cc, w_mod, b_mod).reshape(DEPTH, MOD_ROWS, 3, 1, D_MODEL)

    tm, tn = 512, 512
    fin_g = final_g.reshape(1, D_MODEL)
    final_fused = (DEPTH - 1) % N_MIXERS == CONV_ID
    xs, cs = x, ctx
    for i in range(DEPTH):
        kind, j = i % N_MIXERS, i // N_MIXERS
        ctx_live = i <= last_ctx_reader
        ctx_update = i < last_ctx_reader
        mods = mods_all[i]
        g = norm_g[i].reshape(1, D_MODEL)
        last_layer = i == DEPTH - 1
        streams = [(xs, None, tm)]
        if ctx_update:
            streams.append((cs, ctx_row, ctx_len))
        new = []
        if kind == CONV_ID:
            w_in = conv_w_in[j].astype(BF16)
            w_out = conv_w_out[j].astype(BF16)
            for s, row, tms in streams:
                v, z = _conv_in(s, mods, g, w_in, conv_dw_w[j], conv_dw_b[j].reshape(1, E_WIDE),
                                mod_row=row, tm=tms, tn=tn)
                new.append(_conv_out(v, z, conv_ln_g[j].reshape(1, E_WIDE),
                                     conv_ln_b[j].reshape(1, E_WIDE), w_out, s, mods, fin_g,
                                     mod_row=row, tm=min(tms, 256),
                                     final_norm=last_layer and row is None))
        elif kind == POOL_ID:
            w_in = pool_w_in[j].astype(BF16)
            w_grp = pool_w_grp[j].astype(BF16)
            w_out = pool_w_out[j].astype(BF16)
            for s, row, tms in streams:
                act = _pool_in(s, mods, g, w_in, w_grp, pool_scale[j].reshape(1, E_WIDE),
                               mod_row=row, tm=tms)
                new.append(_proj_out(act, w_out, s, mods, mod_row=row, tm=tms))
        else:
            lambda_init = 0.8 - 0.6 * math.exp(-0.3 * i)
            w_in = att_w_in[j].astype(BF16)
            w_out = att_w_out[j].astype(BF16)
            qkvz = _att_in(xs, mods, g, w_in, mod_row=None, tm=tm, tn=ATT_WIDTH,
                           col_lo=0, n_cols=4 * ATT_WIDTH, rope=True)
            kvc = _att_in(cs, mods, g, w_in, mod_row=ctx_row, tm=ctx_len, tn=ATT_WIDTH,
                          col_lo=ATT_WIDTH, n_cols=2 * ATT_WIDTH, rope=False)
            lam_params = [p[j].reshape(1, ATT_HEAD_DIM)
                          for p in (att_lam_q1, att_lam_k1, att_lam_q2, att_lam_k2)]
            y = _flash(qkvz, kvc, lam_params, att_subln_g[j].reshape(1, HEAD_W),
                       lambda_init=lambda_init, tq=512, tk=512)
            new.append(_proj_out(y, w_out, xs, mods, mod_row=None, tm=tm))
            if ctx_update:
                raise NotImplementedError("context-side queries are not needed at this depth")
        xs = new[0]
        if ctx_update:
            cs = new[1]
        del ctx_live
    return xs if final_fused else _final_norm(xs, fin_g, tm=tm)
```

```python
import functools
import math

import jax
import jax.numpy as jnp
from jax import lax
from jax.experimental import pallas as pl
from jax.experimental.pallas import tpu as pltpu

D_MODEL = 2048
DEPTH = 4
GRID_W = 64
N_MIXERS = 3
CONV_ID, POOL_ID, ATT_ID = 0, 1, 2
E_WIDE = 2 * D_MODEL
CONV_K = 31
CONV_HALF = CONV_K // 2
POOL_WINDOWS = (2, 4, 8, 16)
POOL_GROUP = E_WIDE // len(POOL_WINDOWS)
ATT_WIDTH = D_MODEL
ATT_HEAD_DIM = 64
HEAD_W = 2 * ATT_HEAD_DIM
N_ATT_HEADS = ATT_WIDTH // HEAD_W
ROPE_BASE = 10000.0
EPS = 1e-6
SUBLN_EPS = 1e-5

LANES = 128
HALO = 16
MOD_ROWS = 8
VMEM_LIMIT = 56 * 1024 * 1024

F32 = jnp.float32
BF16 = jnp.bfloat16


def _silu(v):
    return v * jax.nn.sigmoid(v)


def _params(*sem, flags=None):
    return pltpu.CompilerParams(dimension_semantics=sem, vmem_limit_bytes=VMEM_LIMIT, flags=flags)


def _norm_mod(xv, g, sc, sh):
    xv = xv.astype(F32)
    y = xv * lax.rsqrt(jnp.mean(xv * xv, axis=-1, keepdims=True) + EPS)
    return ((y * g) * (1.0 + sc) + sh).astype(BF16)


def _mod_kernel(c_ref, w_ref, b_ref, o_ref):
    s = _silu(c_ref[...])
    o_ref[...] = jnp.dot(s, w_ref[...], precision=lax.Precision.HIGHEST,
                         preferred_element_type=F32) + b_ref[...]


def _modulation(cc, w_mod, b_mod):
    tn = 1024
    n = 3 * D_MODEL
    return pl.pallas_call(
        _mod_kernel,
        grid=(DEPTH, n // tn),
        in_specs=[
            pl.BlockSpec((MOD_ROWS, D_MODEL), lambda l, j: (0, 0)),
            pl.BlockSpec((None, D_MODEL, tn), lambda l, j: (l, 0, j)),
            pl.BlockSpec((None, 1, tn), lambda l, j: (l, 0, j)),
        ],
        out_specs=pl.BlockSpec((None, MOD_ROWS, tn), lambda l, j: (l, 0, j)),
        out_shape=jax.ShapeDtypeStruct((DEPTH, MOD_ROWS, n), F32),
        compiler_params=_params("parallel", "parallel"),
        name="modulation",
    )(cc, w_mod, b_mod.reshape(DEPTH, 1, n))


def _x_specs(tm, seq_len, mod_row):
    nh = tm // HALO
    last = seq_len // HALO - 1
    row = (lambda b: b) if mod_row is None else (lambda b: mod_row)
    return [
        pl.BlockSpec((None, HALO, D_MODEL), lambda b, i, j: (b, jnp.maximum(i * nh - 1, 0), 0)),
        pl.BlockSpec((None, tm, D_MODEL), lambda b, i, j: (b, i, 0)),
        pl.BlockSpec((None, HALO, D_MODEL), lambda b, i, j: (b, jnp.minimum((i + 1) * nh, last), 0)),
        pl.BlockSpec((None, None, 1, D_MODEL), lambda b, i, j: (row(b), 0, 0, 0)),
        pl.BlockSpec((None, None, 1, D_MODEL), lambda b, i, j: (row(b), 1, 0, 0)),
        pl.BlockSpec((1, D_MODEL), lambda b, i, j: (0, 0)),
    ]


def _fill_h(h_scr, xp_ref, x_ref, xn_ref, sh_ref, sc_ref, g_ref, tm):
    g, sc, sh = g_ref[...], sc_ref[...], sh_ref[...]
    h_scr[0:HALO, :] = _norm_mod(xp_ref[...], g, sc, sh)
    h_scr[HALO:HALO + tm, :] = _norm_mod(x_ref[...], g, sc, sh)
    h_scr[HALO + tm:, :] = _norm_mod(xn_ref[...], g, sc, sh)


def _seq_row_mask(i, tm, seq_len, tn):
    row = lax.broadcasted_iota(jnp.int32, (tm + 2 * HALO, tn), 0) + (i * tm - HALO)
    return (row >= 0) & (row < seq_len)


SUB = 8
CONV_GROUPS = 4
SHIFT_ROWS = 64


def _conv_in_kernel(xp_ref, x_ref, xn_ref, sh_ref, sc_ref, g_ref, wa_ref, wb_ref, wz_ref,
                    dw_ref, db_ref, v_ref, z_ref, h_scr, u_scr, us_scr, wt_scr, *, tm, tn, seq_len):
    i, j = pl.program_id(1), pl.program_id(2)
    ext = tm + 2 * HALO
    n_grp = ext // SUB
    hw = tn // 2

    @pl.when(j == 0)
    def _():
        _fill_h(h_scr, xp_ref, x_ref, xn_ref, sh_ref, sc_ref, g_ref, tm)

    for k in range(CONV_K):
        wt_scr[k] = jnp.broadcast_to(dw_ref[k:k + 1, :], (SUB, tn))
    bias = jnp.broadcast_to(db_ref[...], (SUB, tn))
    mask = _seq_row_mask(i, tm, seq_len, hw)
    h = h_scr[...]
    halves = (slice(0, hw), slice(hw, tn))

    def glu(cs):
        a = jnp.dot(h, wa_ref[:, cs], preferred_element_type=F32)
        b = jnp.dot(h, wb_ref[:, cs], preferred_element_type=F32)
        u = jnp.where(mask, a * jax.nn.sigmoid(b), 0.0)
        u_scr[:, cs] = u
        us_scr[0, :, :, cs] = u.reshape(n_grp, SUB, hw)

    def conv(cs):
        for r in range(1, SUB):
            for c0 in range(0, ext - SUB, SHIFT_ROWS):
                n = min(SHIFT_ROWS, ext - SUB - c0)
                us_scr[r, c0 // SUB:(c0 + n) // SUB, :, cs] = (
                    u_scr[pl.ds(c0 + r, n), cs].reshape(n // SUB, SUB, hw))
        for g0 in range(0, tm // SUB, CONV_GROUPS):
            acc = jnp.broadcast_to(bias[:, cs], (CONV_GROUPS, SUB, hw))
            for k in range(CONV_K):
                off = HALO - CONV_HALF + k
                lo = g0 + off // SUB
                acc = acc + us_scr[off % SUB, lo:lo + CONV_GROUPS, :, cs] * wt_scr[k, :, cs]
            v_ref[g0 * SUB:(g0 + CONV_GROUPS) * SUB, cs] = (
                acc.reshape(CONV_GROUPS * SUB, hw).astype(v_ref.dtype))

    glu(halves[0])
    glu(halves[1])
    conv(halves[0])
    z_ref[...] = jnp.dot(h_scr[HALO:HALO + tm, :], wz_ref[...],
                         preferred_element_type=F32).astype(z_ref.dtype)
    conv(halves[1])


def _conv_in(x, mods, norm_g, w_in, dw_w, dw_b, *, mod_row, tm, tn):
    bsz, seq_len, _ = x.shape
    nj = E_WIDE // tn
    kern = functools.partial(_conv_in_kernel, tm=tm, tn=tn, seq_len=seq_len)
    out = jax.ShapeDtypeStruct((bsz, seq_len, E_WIDE), BF16)
    return pl.pallas_call(
        kern,
        grid=(bsz, seq_len // tm, nj),
        in_specs=_x_specs(tm, seq_len, mod_row) + [
            pl.BlockSpec((D_MODEL, tn), lambda b, i, j: (0, j)),
            pl.BlockSpec((D_MODEL, tn), lambda b, i, j: (0, nj + j)),
            pl.BlockSpec((D_MODEL, tn), lambda b, i, j: (0, 2 * nj + j)),
            pl.BlockSpec((CONV_K, tn), lambda b, i, j: (0, j)),
            pl.BlockSpec((1, tn), lambda b, i, j: (0, j)),
        ],
        out_specs=[pl.BlockSpec((None, tm, tn), lambda b, i, j: (b, i, j))] * 2,
        out_shape=[out, out],
        scratch_shapes=[pltpu.VMEM((tm + 2 * HALO, D_MODEL), BF16),
                        pltpu.VMEM((tm + 2 * HALO, tn), F32),
                        pltpu.VMEM((SUB, (tm + 2 * HALO) // SUB, SUB, tn), F32),
                        pltpu.VMEM((CONV_K, SUB, tn), F32)],
        compiler_params=_params("parallel", "parallel", "arbitrary"),
        name="conv_in",
    )(x, x, x, mods, mods, norm_g, w_in, w_in, w_in, dw_w, dw_b)


LN_ROWS = 16
OUT_KC = 1024


def _rmsnorm_rows(xv, g):
    return xv * lax.rsqrt(jnp.mean(xv * xv, axis=-1, keepdims=True) + EPS) * g


def _conv_out_kernel(v_ref, z_ref, lg_ref, lb_ref, w_ref, x_ref, gt_ref, fg_ref, o_ref, *, tm, final_norm):
    mus, rss = [], []
    for r in range(0, tm, LN_ROWS):
        v = v_ref[r:r + LN_ROWS, :].astype(F32)
        mu = jnp.mean(v, axis=-1, keepdims=True)
        d = v - mu
        mus.append(mu)
        rss.append(lax.rsqrt(jnp.mean(d * d, axis=-1, keepdims=True) + EPS))
    mu, rs = jnp.concatenate(mus, axis=0), jnp.concatenate(rss, axis=0)

    acc = None
    for k0 in range(0, E_WIDE, OUT_KC):
        ks = slice(k0, k0 + OUT_KC)
        y = _silu((v_ref[:, ks].astype(F32) - mu) * rs * lg_ref[:, ks] + lb_ref[:, ks])
        act = (y * _silu(z_ref[:, ks].astype(F32))).astype(BF16)
        part = jnp.dot(act, w_ref[ks, :], preferred_element_type=F32)
        acc = part if acc is None else acc + part
    o = x_ref[...] + gt_ref[...] * acc
    o_ref[...] = _rmsnorm_rows(o, fg_ref[...]) if final_norm else o


def _resident(shape):
    return pl.BlockSpec(shape, lambda *_: (0,) * len(shape), pipeline_mode=pl.Buffered(1))


def _conv_out(v, z, ln_g, ln_b, w_out, x, mods, final_g, *, mod_row, tm, final_norm):
    bsz, seq_len, _ = x.shape
    row = (lambda b: b) if mod_row is None else (lambda b: mod_row)
    kern = functools.partial(_conv_out_kernel, tm=tm, final_norm=final_norm)
    return pl.pallas_call(
        kern,
        grid=(bsz, seq_len // tm),
        in_specs=[
            pl.BlockSpec((None, tm, E_WIDE), lambda b, i: (b, i, 0)),
            pl.BlockSpec((None, tm, E_WIDE), lambda b, i: (b, i, 0)),
            _resident((1, E_WIDE)),
            _resident((1, E_WIDE)),
            _resident((E_WIDE, D_MODEL)),
            pl.BlockSpec((None, tm, D_MODEL), lambda b, i: (b, i, 0)),
            pl.BlockSpec((None, None, 1, D_MODEL), lambda b, i: (row(b), 2, 0, 0)),
            _resident((1, D_MODEL)),
        ],
        out_specs=pl.BlockSpec((None, tm, D_MODEL), lambda b, i: (b, i, 0)),
        out_shape=jax.ShapeDtypeStruct(x.shape, F32),
        compiler_params=_params("parallel", "parallel"),
        name="conv_out",
    )(v, z, ln_g, ln_b, w_out, x, mods, final_g)


POOL_ROWS = 64


def _pool_in_kernel(xp_ref, x_ref, xn_ref, sh_ref, sc_ref, g_ref, wu_ref, wz_ref, wg_ref,
                    ps_ref, o_ref, h_scr, u_scr, p_scr, *, tm, seq_len):
    i, grp = pl.program_id(1), pl.program_id(2)
    tn = POOL_GROUP

    @pl.when(grp == 0)
    def _():
        _fill_h(h_scr, xp_ref, x_ref, xn_ref, sh_ref, sc_ref, g_ref, tm)

    u = jnp.dot(h_scr[...], wu_ref[...], preferred_element_type=F32)
    u_scr[...] = jnp.where(_seq_row_mask(i, tm, seq_len, tn), u, 0.0)

    for gi, win in enumerate(POOL_WINDOWS):
        @pl.when(grp == gi)
        def _(win=win):
            half = win // 2

            for r0 in range(0, tm, POOL_ROWS):
                t = lax.broadcasted_iota(jnp.int32, (POOL_ROWS, 1), 0) + (i * tm + r0)
                cnt = jnp.minimum(t + half, seq_len) - jnp.maximum(t - half, 0)
                acc = u_scr[pl.ds(r0 + (HALO - half), POOL_ROWS), :]
                for d in range(1, win):
                    acc = acc + u_scr[pl.ds(r0 + (HALO - half + d), POOL_ROWS), :]
                p = acc / cnt.astype(F32) - u_scr[pl.ds(r0 + HALO, POOL_ROWS), :]
                p_scr[pl.ds(r0, POOL_ROWS), :] = p.astype(BF16)

    z = jnp.dot(h_scr[HALO:HALO + tm, :], wz_ref[...], preferred_element_type=F32)
    q = jnp.dot(p_scr[...], wg_ref[...], preferred_element_type=F32)
    o_ref[...] = (q * ps_ref[...] * _silu(z)).astype(o_ref.dtype)


def _pool_in(x, mods, norm_g, w_in, w_grp, scale, *, mod_row, tm):
    bsz, seq_len, _ = x.shape
    ng = len(POOL_WINDOWS)
    tn = POOL_GROUP
    kern = functools.partial(_pool_in_kernel, tm=tm, seq_len=seq_len)
    return pl.pallas_call(
        kern,
        grid=(bsz, seq_len // tm, ng),
        in_specs=_x_specs(tm, seq_len, mod_row) + [
            pl.BlockSpec((D_MODEL, tn), lambda b, i, j: (0, j)),
            pl.BlockSpec((D_MODEL, tn), lambda b, i, j: (0, ng + j)),
            pl.BlockSpec((None, tn, tn), lambda b, i, j: (j, 0, 0)),
            pl.BlockSpec((1, tn), lambda b, i, j: (0, j)),
        ],
        out_specs=pl.BlockSpec((None, tm, tn), lambda b, i, j: (b, i, j)),
        out_shape=jax.ShapeDtypeStruct((bsz, seq_len, E_WIDE), BF16),
        scratch_shapes=[pltpu.VMEM((tm + 2 * HALO, D_MODEL), BF16),
                        pltpu.VMEM((tm + 2 * HALO, tn), F32),
                        pltpu.VMEM((tm, tn), BF16)],
        compiler_params=_params("parallel", "parallel", "arbitrary"),
        name="pool_in",
    )(x, x, x, mods, mods, norm_g, w_in, w_in, w_grp, scale)


def _proj_out_kernel(a_ref, w_ref, x_ref, gt_ref, o_ref):
    y = jnp.dot(a_ref[...], w_ref[...], preferred_element_type=F32)
    o_ref[...] = x_ref[...] + gt_ref[...] * y


def _proj_out(act, w_out, x, mods, *, mod_row, tm):
    bsz, seq_len, _ = x.shape
    kdim = act.shape[-1]
    row = (lambda b: b) if mod_row is None else (lambda b: mod_row)
    return pl.pallas_call(
        _proj_out_kernel,
        grid=(bsz, seq_len // tm),
        in_specs=[
            pl.BlockSpec((None, tm, kdim), lambda b, i: (b, i, 0)),
            _resident((kdim, D_MODEL)),
            pl.BlockSpec((None, tm, D_MODEL), lambda b, i: (b, i, 0)),
            pl.BlockSpec((None, None, 1, D_MODEL), lambda b, i: (row(b), 2, 0, 0)),
        ],
        out_specs=pl.BlockSpec((None, tm, D_MODEL), lambda b, i: (b, i, 0)),
        out_shape=jax.ShapeDtypeStruct(x.shape, F32),
        compiler_params=_params("parallel", "parallel"),
        name="proj_out",
    )(act, w_out, x, mods)


QUERY_SCALE = ATT_HEAD_DIM ** -0.5 * math.log2(math.e)
FLASH_ROWS = 32


def _rope_tables(seq_len):
    t = jnp.arange(seq_len)
    n_freq = ATT_HEAD_DIM // 4
    freqs = ROPE_BASE ** (-jnp.arange(n_freq, dtype=F32) / n_freq)
    pos = jnp.stack([t // GRID_W, t % GRID_W], axis=1).astype(F32)
    ang = pos[:, :, None] * freqs
    cos, sin = jnp.cos(ang), jnp.sin(ang)
    zero = jnp.zeros_like(sin)

    def table(first, second):
        tab = jnp.stack([first, second], axis=2)
        return jnp.tile(tab.reshape(seq_len, ATT_HEAD_DIM), (1, 2))

    return table(cos, cos), table(-sin, zero), table(zero, sin)


def _att_in_kernel(xp_ref, x_ref, xn_ref, sh_ref, sc_ref, g_ref, w_ref, cos_ref, sa_ref, sb_ref,
                   o_ref, h_scr, *, tm, tn, n_q_tiles, n_rope_tiles):
    del xp_ref, xn_ref
    j = pl.program_id(2)

    @pl.when(j == 0)
    def _():
        h_scr[...] = _norm_mod(x_ref[...], g_ref[...], sc_ref[...], sh_ref[...])

    @pl.when(j < n_rope_tiles)
    def _():
        y = jnp.dot(h_scr[...], w_ref[...], preferred_element_type=F32)
        scale = jnp.where(j < n_q_tiles, QUERY_SCALE, 1.0)
        cos, sa, sb = cos_ref[...] * scale, sa_ref[...] * scale, sb_ref[...] * scale
        half = ATT_HEAD_DIM // 4
        for c in range(tn // LANES):
            yc = y[:, c * LANES:(c + 1) * LANES]
            r = (yc * cos + pltpu.roll(yc, LANES - half, axis=1) * sa
                 + pltpu.roll(yc, half, axis=1) * sb)
            o_ref[:, c * LANES:(c + 1) * LANES] = r.astype(o_ref.dtype)

    @pl.when(j >= n_rope_tiles)
    def _():
        o_ref[...] = jnp.dot(h_scr[...], w_ref[...], preferred_element_type=F32).astype(o_ref.dtype)


def _att_in(x, mods, norm_g, w_in, *, mod_row, tm, tn, col_lo, n_cols, rope):
    bsz, seq_len, _ = x.shape
    j0 = col_lo // tn
    if rope:
        tabs = _rope_tables(seq_len)
        n_q, n_rope = ATT_WIDTH // tn, 2 * ATT_WIDTH // tn
    else:
        tabs = (jnp.zeros((seq_len, LANES), F32),) * 3
        n_q = n_rope = 0
    kern = functools.partial(_att_in_kernel, tm=tm, tn=tn, n_q_tiles=n_q, n_rope_tiles=n_rope)
    tab_spec = pl.BlockSpec((tm, LANES), lambda b, i, j: (i, 0))
    return pl.pallas_call(
        kern,
        grid=(bsz, seq_len // tm, n_cols // tn),
        in_specs=_x_specs(tm, seq_len, mod_row) + [
            pl.BlockSpec((D_MODEL, tn), lambda b, i, j: (0, j0 + j)),
            tab_spec, tab_spec, tab_spec,
        ],
        out_specs=pl.BlockSpec((None, tm, tn), lambda b, i, j: (b, i, j)),
        out_shape=jax.ShapeDtypeStruct((bsz, seq_len, n_cols), BF16),
        scratch_shapes=[pltpu.VMEM((tm, D_MODEL), BF16)],
        compiler_params=_params("parallel", "parallel", "arbitrary"),
        name="att_in",
    )(x, x, x, mods, mods, norm_g, w_in, *tabs)


def _flash_kernel(lq1_ref, lk1_ref, lq2_ref, lk2_ref, sg_ref, q_ref, k_ref, v_ref, kc_ref, vc_ref,
                  z_ref, y_ref, m_scr, acc_scr, s_scr, p_scr, a_scr, sc_scr, pc_scr,
                  *, tq, tk, seq_len, lambda_init):
    q = q_ref[...]
    lane = lax.broadcasted_iota(jnp.int32, q.shape, 1)
    zero = jnp.zeros_like(q)
    qq = jnp.concatenate([jnp.where(lane < ATT_HEAD_DIM, q, zero),
                          jnp.where(lane >= ATT_HEAD_DIM, q, zero)], axis=0)

    m_scr[...] = jnp.full(m_scr.shape, -jnp.inf, F32)
    acc_scr[...] = jnp.zeros(acc_scr.shape, F32)
    n_chunks = seq_len // tk
    nt = (((1,), (1,)), ((), ()))

    def scores(kb):
        return lax.dot_general(qq, kb, nt, preferred_element_type=F32)

    def softmax(blocks, a_ref):
        for r in range(0, 2 * tq, FLASH_ROWS):
            rows = slice(r, r + FLASH_ROWS)
            cols = [s_ref[rows, j:j + LANES] for s_ref, _, width in blocks for j in range(0, width, LANES)]
            m_old = m_scr[rows, :]
            m_new = jnp.maximum(m_old, jnp.max(functools.reduce(jnp.maximum, cols), axis=1, keepdims=True))
            a_ref[rows, :] = jnp.exp2(m_old - m_new)
            m_scr[rows, :] = m_new
            for s_ref, p_ref, width in blocks:
                for j in range(0, width, LANES):
                    p_ref[rows, j:j + LANES] = jnp.exp2((s_ref[rows, j:j + LANES] - m_new).astype(BF16))

    def with_ones(vb):
        return jnp.concatenate([vb, jnp.ones_like(vb)], axis=1)

    def accumulate(a_ref, terms):
        pv = [jnp.dot(p_ref[...], with_ones(vb), preferred_element_type=F32) for p_ref, vb in terms]
        total, alpha = functools.reduce(jnp.add, pv), a_ref[...]
        for cs in (slice(0, HEAD_W), slice(HEAD_W, 2 * HEAD_W)):
            acc_scr[:, cs] = alpha * acc_scr[:, cs] + total[:, cs]

    def k_chunk(c):
        return k_ref[c * tk:(c + 1) * tk, :]

    def v_chunk(c):
        return v_ref[c * tk:(c + 1) * tk, :]

    ctx_w = kc_ref.shape[0]
    sc_scr[...] = scores(kc_ref[...])
    s_scr[0] = scores(k_chunk(0))
    softmax([(s_scr.at[0], p_scr.at[0], tk)], a_scr.at[0])
    s_scr[1] = scores(k_chunk(1))

    def stage(c, cur, prv, last):
        if not last:
            s_scr[prv] = scores(k_chunk(c + 1))
        accumulate(a_scr.at[prv], [(p_scr.at[prv], v_chunk(c - 1))])
        blocks = [(s_scr.at[cur], p_scr.at[cur], tk)]
        if last:
            blocks.append((sc_scr, pc_scr, ctx_w))
        softmax(blocks, a_scr.at[cur])

    for c in range(1, n_chunks - 1):
        stage(c, c % 2, (c + 1) % 2, False)
    stage(n_chunks - 1, 1, 0, True)
    accumulate(a_scr.at[1], [(p_scr.at[1], v_chunk(n_chunks - 1)), (pc_scr, vc_ref[...])])

    lam = (jnp.exp(jnp.sum(lq1_ref[...] * lk1_ref[...], axis=1, keepdims=True))
           - jnp.exp(jnp.sum(lq2_ref[...] * lk2_ref[...], axis=1, keepdims=True)) + lambda_init)
    o1 = acc_scr[0:tq, 0:HEAD_W] / acc_scr[0:tq, HEAD_W:]
    o2 = acc_scr[tq:, 0:HEAD_W] / acc_scr[tq:, HEAD_W:]
    o = o1 - lam * o2
    o = o * lax.rsqrt(jnp.mean(o * o, axis=-1, keepdims=True) + SUBLN_EPS)
    o = (o * sg_ref[...]) * (1.0 - lambda_init)
    y_ref[...] = (o * _silu(z_ref[...].astype(F32))).astype(y_ref.dtype)


def _flash(qkvz, kvc, lam_params, subln_g, *, lambda_init, tq, tk):
    bsz, seq_len, _ = qkvz.shape
    ctx_len = kvc.shape[1]
    nh = N_ATT_HEADS
    assert seq_len % (2 * tk) == 0 and ctx_len % LANES == 0
    kern = functools.partial(_flash_kernel, tq=tq, tk=tk, seq_len=seq_len, lambda_init=lambda_init)
    vec = pl.BlockSpec((1, ATT_HEAD_DIM), lambda b, h, i: (0, 0))
    return pl.pallas_call(
        kern,
        grid=(bsz, nh, seq_len // tq),
        in_specs=[vec, vec, vec, vec,
                  pl.BlockSpec((1, HEAD_W), lambda b, h, i: (0, 0)),
                  pl.BlockSpec((None, tq, HEAD_W), lambda b, h, i: (b, i, h)),
                  pl.BlockSpec((None, seq_len, HEAD_W), lambda b, h, i: (b, 0, nh + h)),
                  pl.BlockSpec((None, seq_len, HEAD_W), lambda b, h, i: (b, 0, 2 * nh + h)),
                  pl.BlockSpec((None, ctx_len, HEAD_W), lambda b, h, i: (b, 0, h)),
                  pl.BlockSpec((None, ctx_len, HEAD_W), lambda b, h, i: (b, 0, nh + h)),
                  pl.BlockSpec((None, tq, HEAD_W), lambda b, h, i: (b, i, 3 * nh + h))],
        out_specs=pl.BlockSpec((None, tq, HEAD_W), lambda b, h, i: (b, i, h)),
        out_shape=jax.ShapeDtypeStruct((bsz, seq_len, ATT_WIDTH), BF16),
        scratch_shapes=[pltpu.VMEM((2 * tq, LANES), F32), pltpu.VMEM((2 * tq, 2 * HEAD_W), F32),
                        pltpu.VMEM((2, 2 * tq, tk), F32), pltpu.VMEM((2, 2 * tq, tk), BF16),
                        pltpu.VMEM((2, 2 * tq, LANES), F32),
                        pltpu.VMEM((2 * tq, ctx_len), F32), pltpu.VMEM((2 * tq, ctx_len), BF16)],
        compiler_params=_params("parallel", "parallel", "arbitrary"),
        name="diff_flash",
    )(*lam_params, subln_g, qkvz, qkvz, qkvz, kvc, kvc, qkvz)


def _final_norm_kernel(x_ref, g_ref, o_ref):
    xv = x_ref[...]
    o_ref[...] = xv * lax.rsqrt(jnp.mean(xv * xv, axis=-1, keepdims=True) + EPS) * g_ref[...]


def _final_norm(x, g, *, tm):
    bsz, seq_len, _ = x.shape
    return pl.pallas_call(
        _final_norm_kernel,
        grid=(bsz, seq_len // tm),
        in_specs=[pl.BlockSpec((None, tm, D_MODEL), lambda b, i: (b, i, 0)),
                  pl.BlockSpec((1, D_MODEL), lambda b, i: (0, 0))],
        out_specs=pl.BlockSpec((None, tm, D_MODEL), lambda b, i: (b, i, 0)),
        out_shape=jax.ShapeDtypeStruct(x.shape, F32),
        compiler_params=_params("parallel", "parallel"),
        name="final_norm",
    )(x, g)


def kernel(x, c, ctx, c_ctx, norm_g, w_mod, b_mod, final_g, conv_w_in, conv_dw_w, conv_dw_b,
           conv_ln_g, conv_ln_b, conv_w_out, pool_w_in, pool_w_grp, pool_scale, pool_w_out,
           att_w_in, att_lam_q1, att_lam_k1, att_lam_q2, att_lam_k2, att_subln_g, att_w_out):
    bsz = x.shape[0]
    ctx_len = ctx.shape[1]
    n_att = DEPTH // N_MIXERS
    last_ctx_reader = N_MIXERS * (n_att - 1) + ATT_ID if n_att > 0 else -1
    ctx_row = bsz

    cc = jnp.zeros((MOD_ROWS, D_MODEL), F32).at[:bsz].set(c).at[ctx_row].set(c_ctx)
    mods_all = _modulation(cc, w_mod, b_mod).reshape(DEPTH, MOD_ROWS, 3, 1, D_MODEL)

    tm, tn = 512, 512
    fin_g = final_g.reshape(1, D_MODEL)
    final_fused = (DEPTH - 1) % N_MIXERS == CONV_ID
    xs, cs = x, ctx
    for i in range(DEPTH):
        kind, j = i % N_MIXERS, i // N_MIXERS
        ctx_live = i <= last_ctx_reader
        ctx_update = i < last_ctx_reader
        mods = mods_all[i]
        g = norm_g[i].reshape(1, D_MODEL)
        last_layer = i == DEPTH - 1
        streams = [(xs, None, tm)]
        if ctx_update:
            streams.append((cs, ctx_row, ctx_len))
        new = []
        if kind == CONV_ID:
            w_in = conv_w_in[j].astype(BF16)
            w_out = conv_w_out[j].astype(BF16)
            for s, row, tms in streams:
                v, z = _conv_in(s, mods, g, w_in, conv_dw_w[j], conv_dw_b[j].reshape(1, E_WIDE),
                                mod_row=row, tm=tms, tn=tn)
                new.append(_conv_out(v, z, conv_ln_g[j].reshape(1, E_WIDE),
                                     conv_ln_b[j].reshape(1, E_WIDE), w_out, s, mods, fin_g,
                                     mod_row=row, tm=min(tms, 256),
                                     final_norm=last_layer and row is None))
        elif kind == POOL_ID:
            w_in = pool_w_in[j].astype(BF16)
            w_grp = pool_w_grp[j].astype(BF16)
            w_out = pool_w_out[j].astype(BF16)
            for s, row, tms in streams:
                act = _pool_in(s, mods, g, w_in, w_grp, pool_scale[j].reshape(1, E_WIDE),
                               mod_row=row, tm=tms)
                new.append(_proj_out(act, w_out, s, mods, mod_row=row, tm=tms))
        else:
            lambda_init = 0.8 - 0.6 * math.exp(-0.3 * i)
            w_in = att_w_in[j].astype(BF16)
            w_out = att_w_out[j].astype(BF16)
            qkvz = _att_in(xs, mods, g, w_in, mod_row=None, tm=tm, tn=ATT_WIDTH,
                           col_lo=0, n_cols=4 * ATT_WIDTH, rope=True)
            kvc = _att_in(cs, mods, g, w_in, mod_row=ctx_row, tm=ctx_len, tn=ATT_WIDTH,
                          col_lo=ATT_WIDTH, n_cols=2 * ATT_WIDTH, rope=False)
            lam_params = [p[j].reshape(1, ATT_HEAD_DIM)
                          for p in (att_lam_q1, att_lam_k1, att_lam_q2, att_lam_k2)]
            y = _flash(qkvz, kvc, lam_params, att_subln_g[j].reshape(1, HEAD_W),
                       lambda_init=lambda_init, tq=512, tk=512)
            new.append(_proj_out(y, w_out, xs, mods, mod_row=None, tm=tm))
            if ctx_update:
                raise NotImplementedError("context-side queries are not needed at this depth")
        xs = new[0]
        if ctx_update:
            cs = new[1]
        del ctx_live
    return xs if final_fused else _final_norm(xs, fin_g, tm=tm)
```

```python
import functools
import math
from typing import NamedTuple

import jax
import jax.numpy as jnp
from jax import lax
from jax.experimental import pallas as pl
from jax.experimental.pallas import tpu as pltpu

D_MODEL = 2048
DEPTH = 4
GRID_W = 64
N_MIXERS = 3
CONV_ID, POOL_ID, ATT_ID = 0, 1, 2
E_WIDE = 2 * D_MODEL
CONV_K = 31
CONV_HALF = CONV_K // 2
POOL_WINDOWS = (2, 4, 8, 16)
POOL_GROUP = E_WIDE // len(POOL_WINDOWS)
ATT_WIDTH = D_MODEL
ATT_HEAD_DIM = 64
HEAD_W = 2 * ATT_HEAD_DIM
N_ATT_HEADS = ATT_WIDTH // HEAD_W
ROPE_BASE = 10000.0
EPS = 1e-6
SUBLN_EPS = 1e-5

LANES = 128
HALO = 16
MOD_ROWS = 8
VMEM_LIMIT = 56 * 1024 * 1024

F32 = jnp.float32
BF16 = jnp.bfloat16


class _Tiles(NamedTuple):
    tokens: int = 512
    cols: int = 512
    conv_out_tokens: int = 256
    flash_q: int = 512
    flash_k: int = 512
    mod_cols: int = 1024


TILES = _Tiles()


def _silu(v):
    return v * jax.nn.sigmoid(v)


def _params(*sem, flags=None):
    return pltpu.CompilerParams(dimension_semantics=sem, vmem_limit_bytes=VMEM_LIMIT, flags=flags)


NORM_ROWS = 32


def _norm_mod_rows(dst, r0, x_ref, g_ref, sc_ref, sh_ref):
    gain, sh = g_ref[...] * (1.0 + sc_ref[...]), sh_ref[...]
    n = x_ref.shape[0]
    for c in range(0, n, NORM_ROWS):
        m = min(NORM_ROWS, n - c)
        xv = x_ref[c:c + m, :].astype(F32)
        y = xv * lax.rsqrt(jnp.mean(xv * xv, axis=-1, keepdims=True) + EPS)
        dst[r0 + c:r0 + c + m, :] = (y * gain + sh).astype(BF16)


def _mod_kernel(c_ref, w_ref, b_ref, o_ref):
    s = _silu(c_ref[...])
    o_ref[...] = jnp.dot(s, w_ref[...], precision=lax.Precision.HIGHEST,
                         preferred_element_type=F32) + b_ref[...]


def _modulation(cc, w_mod, b_mod):
    tn = TILES.mod_cols
    n = 3 * D_MODEL
    return pl.pallas_call(
        _mod_kernel,
        grid=(DEPTH, n // tn),
        in_specs=[
            pl.BlockSpec((MOD_ROWS, D_MODEL), lambda l, j: (0, 0)),
            pl.BlockSpec((None, D_MODEL, tn), lambda l, j: (l, 0, j)),
            pl.BlockSpec((None, 1, tn), lambda l, j: (l, 0, j)),
        ],
        out_specs=pl.BlockSpec((None, MOD_ROWS, tn), lambda l, j: (l, 0, j)),
        out_shape=jax.ShapeDtypeStruct((DEPTH, MOD_ROWS, n), F32),
        compiler_params=_params("parallel", "parallel"),
        name="modulation",
    )(cc, w_mod, b_mod.reshape(DEPTH, 1, n))


def _x_specs(tm, seq_len, mod_row):
    nh = tm // HALO
    last = seq_len // HALO - 1
    row = (lambda b: b) if mod_row is None else (lambda b: mod_row)
    return [
        pl.BlockSpec((None, HALO, D_MODEL), lambda b, i, j: (b, jnp.maximum(i * nh - 1, 0), 0)),
        pl.BlockSpec((None, tm, D_MODEL), lambda b, i, j: (b, i, 0)),
        pl.BlockSpec((None, HALO, D_MODEL), lambda b, i, j: (b, jnp.minimum((i + 1) * nh, last), 0)),
        pl.BlockSpec((None, None, 1, D_MODEL), lambda b, i, j: (row(b), 0, 0, 0)),
        pl.BlockSpec((None, None, 1, D_MODEL), lambda b, i, j: (row(b), 1, 0, 0)),
        pl.BlockSpec((1, D_MODEL), lambda b, i, j: (0, 0)),
    ]


def _fill_h(h_scr, xp_ref, x_ref, xn_ref, sh_ref, sc_ref, g_ref, tm):
    _norm_mod_rows(h_scr, 0, xp_ref, g_ref, sc_ref, sh_ref)
    _norm_mod_rows(h_scr, HALO, x_ref, g_ref, sc_ref, sh_ref)
    _norm_mod_rows(h_scr, HALO + tm, xn_ref, g_ref, sc_ref, sh_ref)


def _seq_row_mask(i, tm, seq_len, tn):
    row = lax.broadcasted_iota(jnp.int32, (tm + 2 * HALO, tn), 0) + (i * tm - HALO)
    return (row >= 0) & (row < seq_len)


SUB = 8
CONV_GROUPS = 4
SHIFT_ROWS = 64


def _conv_in_kernel(xp_ref, x_ref, xn_ref, sh_ref, sc_ref, g_ref, wa_ref, wb_ref, wz_ref,
                    dw_ref, db_ref, v_ref, z_ref, h_scr, u_scr, us_scr, wt_scr, *, tm, tn, seq_len):
    i, j = pl.program_id(1), pl.program_id(2)
    ext = tm + 2 * HALO
    n_grp = ext // SUB
    hw = tn // 2

    @pl.when(j == 0)
    def _():
        _fill_h(h_scr, xp_ref, x_ref, xn_ref, sh_ref, sc_ref, g_ref, tm)

    for k in range(CONV_K):
        wt_scr[k] = jnp.broadcast_to(dw_ref[k:k + 1, :], (SUB, tn))
    bias = jnp.broadcast_to(db_ref[...], (SUB, tn))
    mask = _seq_row_mask(i, tm, seq_len, hw)
    h = h_scr[...]
    halves = (slice(0, hw), slice(hw, tn))

    def glu(cs):
        a = jnp.dot(h, wa_ref[:, cs], preferred_element_type=F32)
        b = jnp.dot(h, wb_ref[:, cs], preferred_element_type=F32)
        u = jnp.where(mask, a * jax.nn.sigmoid(b), 0.0)
        u_scr[:, cs] = u
        us_scr[0, :, :, cs] = u.reshape(n_grp, SUB, hw)

    def conv(cs):
        for r in range(1, SUB):
            for c0 in range(0, ext - SUB, SHIFT_ROWS):
                n = min(SHIFT_ROWS, ext - SUB - c0)
                us_scr[r, c0 // SUB:(c0 + n) // SUB, :, cs] = (
                    u_scr[pl.ds(c0 + r, n), cs].reshape(n // SUB, SUB, hw))
        for g0 in range(0, tm // SUB, CONV_GROUPS):
            acc = jnp.broadcast_to(bias[:, cs], (CONV_GROUPS, SUB, hw))
            for k in range(CONV_K):
                off = HALO - CONV_HALF + k
                lo = g0 + off // SUB
                acc = acc + us_scr[off % SUB, lo:lo + CONV_GROUPS, :, cs] * wt_scr[k, :, cs]
            v_ref[g0 * SUB:(g0 + CONV_GROUPS) * SUB, cs] = (
                acc.reshape(CONV_GROUPS * SUB, hw).astype(v_ref.dtype))

    glu(halves[0])
    glu(halves[1])
    conv(halves[0])
    z_ref[...] = jnp.dot(h_scr[HALO:HALO + tm, :], wz_ref[...],
                         preferred_element_type=F32).astype(z_ref.dtype)
    conv(halves[1])


def _conv_in(x, mods, norm_g, w_in, dw_w, dw_b, *, mod_row, tm, tn):
    bsz, seq_len, _ = x.shape
    nj = E_WIDE // tn
    kern = functools.partial(_conv_in_kernel, tm=tm, tn=tn, seq_len=seq_len)
    out = jax.ShapeDtypeStruct((bsz, seq_len, E_WIDE), BF16)
    return pl.pallas_call(
        kern,
        grid=(bsz, seq_len // tm, nj),
        in_specs=_x_specs(tm, seq_len, mod_row) + [
            pl.BlockSpec((D_MODEL, tn), lambda b, i, j: (0, j)),
            pl.BlockSpec((D_MODEL, tn), lambda b, i, j: (0, nj + j)),
            pl.BlockSpec((D_MODEL, tn), lambda b, i, j: (0, 2 * nj + j)),
            pl.BlockSpec((CONV_K, tn), lambda b, i, j: (0, j)),
            pl.BlockSpec((1, tn), lambda b, i, j: (0, j)),
        ],
        out_specs=[pl.BlockSpec((None, tm, tn), lambda b, i, j: (b, i, j))] * 2,
        out_shape=[out, out],
        scratch_shapes=[pltpu.VMEM((tm + 2 * HALO, D_MODEL), BF16),
                        pltpu.VMEM((tm + 2 * HALO, tn), F32),
                        pltpu.VMEM((SUB, (tm + 2 * HALO) // SUB, SUB, tn), F32),
                        pltpu.VMEM((CONV_K, SUB, tn), F32)],
        compiler_params=_params("parallel", "parallel", "arbitrary"),
        name="conv_in",
    )(x, x, x, mods, mods, norm_g, w_in, w_in, w_in, dw_w, dw_b)


LN_ROWS = 16
OUT_KC = 256


def _rmsnorm_rows(xv, g):
    return xv * lax.rsqrt(jnp.mean(xv * xv, axis=-1, keepdims=True) + EPS) * g


def _conv_out_kernel(v_ref, z_ref, lg_ref, lb_ref, w_ref, x_ref, gt_ref, fg_ref, o_ref, *, tm, final_norm):
    mus, rss = [], []
    for r in range(0, tm, LN_ROWS):
        v = v_ref[r:r + LN_ROWS, :].astype(F32)
        mu = jnp.mean(v, axis=-1, keepdims=True)
        d = v - mu
        mus.append(mu)
        rss.append(lax.rsqrt(jnp.mean(d * d, axis=-1, keepdims=True) + EPS))
    mu, rs = jnp.concatenate(mus, axis=0), jnp.concatenate(rss, axis=0)

    acc = None
    for k0 in range(0, E_WIDE, OUT_KC):
        ks = slice(k0, k0 + OUT_KC)
        y = _silu((v_ref[:, ks].astype(F32) - mu) * rs * lg_ref[:, ks] + lb_ref[:, ks])
        act = (y * _silu(z_ref[:, ks].astype(F32))).astype(BF16)
        part = jnp.dot(act, w_ref[ks, :], preferred_element_type=F32)
        acc = part if acc is None else acc + part
    o = x_ref[...] + gt_ref[...] * acc
    o_ref[...] = _rmsnorm_rows(o, fg_ref[...]) if final_norm else o


def _resident(shape):
    return pl.BlockSpec(shape, lambda *_: (0,) * len(shape), pipeline_mode=pl.Buffered(1))


def _conv_out(v, z, ln_g, ln_b, w_out, x, mods, final_g, *, mod_row, tm, final_norm):
    bsz, seq_len, _ = x.shape
    row = (lambda b: b) if mod_row is None else (lambda b: mod_row)
    kern = functools.partial(_conv_out_kernel, tm=tm, final_norm=final_norm)
    return pl.pallas_call(
        kern,
        grid=(bsz, seq_len // tm),
        in_specs=[
            pl.BlockSpec((None, tm, E_WIDE), lambda b, i: (b, i, 0)),
            pl.BlockSpec((None, tm, E_WIDE), lambda b, i: (b, i, 0)),
            _resident((1, E_WIDE)),
            _resident((1, E_WIDE)),
            _resident((E_WIDE, D_MODEL)),
            pl.BlockSpec((None, tm, D_MODEL), lambda b, i: (b, i, 0)),
            pl.BlockSpec((None, None, 1, D_MODEL), lambda b, i: (row(b), 2, 0, 0)),
            _resident((1, D_MODEL)),
        ],
        out_specs=pl.BlockSpec((None, tm, D_MODEL), lambda b, i: (b, i, 0)),
        out_shape=jax.ShapeDtypeStruct(x.shape, F32),
        compiler_params=_params("parallel", "parallel"),
        name="conv_out",
    )(v, z, ln_g, ln_b, w_out, x, mods, final_g)


POOL_ROWS = 64


def _pool_in_kernel(xp_ref, x_ref, xn_ref, sh_ref, sc_ref, g_ref, wu_ref, wz_ref, wg_ref,
                    ps_ref, o_ref, h_scr, u_scr, p_scr, *, tm, seq_len):
    i, grp = pl.program_id(1), pl.program_id(2)
    tn = POOL_GROUP

    @pl.when(grp == 0)
    def _():
        _fill_h(h_scr, xp_ref, x_ref, xn_ref, sh_ref, sc_ref, g_ref, tm)

    u = jnp.dot(h_scr[...], wu_ref[...], preferred_element_type=F32)
    u_scr[...] = jnp.where(_seq_row_mask(i, tm, seq_len, tn), u, 0.0)

    for gi, win in enumerate(POOL_WINDOWS):
        @pl.when(grp == gi)
        def _(win=win):
            half = win // 2

            for r0 in range(0, tm, POOL_ROWS):
                t = lax.broadcasted_iota(jnp.int32, (POOL_ROWS, 1), 0) + (i * tm + r0)
                cnt = jnp.minimum(t + half, seq_len) - jnp.maximum(t - half, 0)
                acc = u_scr[pl.ds(r0 + (HALO - half), POOL_ROWS), :]
                for d in range(1, win):
                    acc = acc + u_scr[pl.ds(r0 + (HALO - half + d), POOL_ROWS), :]
                p = acc / cnt.astype(F32) - u_scr[pl.ds(r0 + HALO, POOL_ROWS), :]
                p_scr[pl.ds(r0, POOL_ROWS), :] = p.astype(BF16)

    z = jnp.dot(h_scr[HALO:HALO + tm, :], wz_ref[...], preferred_element_type=F32)
    q = jnp.dot(p_scr[...], wg_ref[...], preferred_element_type=F32)
    o_ref[...] = (q * ps_ref[...] * _silu(z)).astype(o_ref.dtype)


def _pool_in(x, mods, norm_g, w_in, w_grp, scale, *, mod_row, tm):
    bsz, seq_len, _ = x.shape
    ng = len(POOL_WINDOWS)
    tn = POOL_GROUP
    kern = functools.partial(_pool_in_kernel, tm=tm, seq_len=seq_len)
    return pl.pallas_call(
        kern,
        grid=(bsz, seq_len // tm, ng),
        in_specs=_x_specs(tm, seq_len, mod_row) + [
            pl.BlockSpec((D_MODEL, tn), lambda b, i, j: (0, j)),
            pl.BlockSpec((D_MODEL, tn), lambda b, i, j: (0, ng + j)),
            pl.BlockSpec((None, tn, tn), lambda b, i, j: (j, 0, 0)),
            pl.BlockSpec((1, tn), lambda b, i, j: (0, j)),
        ],
        out_specs=pl.BlockSpec((None, tm, tn), lambda b, i, j: (b, i, j)),
        out_shape=jax.ShapeDtypeStruct((bsz, seq_len, E_WIDE), BF16),
        scratch_shapes=[pltpu.VMEM((tm + 2 * HALO, D_MODEL), BF16),
                        pltpu.VMEM((tm + 2 * HALO, tn), F32),
                        pltpu.VMEM((tm, tn), BF16)],
        compiler_params=_params("parallel", "parallel", "arbitrary"),
        name="pool_in",
    )(x, x, x, mods, mods, norm_g, w_in, w_in, w_grp, scale)


def _proj_out_kernel(a_ref, w_ref, x_ref, gt_ref, o_ref):
    y = jnp.dot(a_ref[...], w_ref[...], preferred_element_type=F32)
    o_ref[...] = x_ref[...] + gt_ref[...] * y


def _proj_out(act, w_out, x, mods, *, mod_row, tm):
    bsz, seq_len, _ = x.shape
    kdim = act.shape[-1]
    row = (lambda b: b) if mod_row is None else (lambda b: mod_row)
    return pl.pallas_call(
        _proj_out_kernel,
        grid=(bsz, seq_len // tm),
        in_specs=[
            pl.BlockSpec((None, tm, kdim), lambda b, i: (b, i, 0)),
            _resident((kdim, D_MODEL)),
            pl.BlockSpec((None, tm, D_MODEL), lambda b, i: (b, i, 0)),
            pl.BlockSpec((None, None, 1, D_MODEL), lambda b, i: (row(b), 2, 0, 0)),
        ],
        out_specs=pl.BlockSpec((None, tm, D_MODEL), lambda b, i: (b, i, 0)),
        out_shape=jax.ShapeDtypeStruct(x.shape, F32),
        compiler_params=_params("parallel", "parallel"),
        name="proj_out",
    )(act, w_out, x, mods)


QUERY_SCALE = ATT_HEAD_DIM ** -0.5 * math.log2(math.e)
FLASH_ROWS = 32


def _rope_tables(seq_len):
    t = jnp.arange(seq_len)
    n_freq = ATT_HEAD_DIM // 4
    freqs = ROPE_BASE ** (-jnp.arange(n_freq, dtype=F32) / n_freq)
    pos = jnp.stack([t // GRID_W, t % GRID_W], axis=1).astype(F32)
    ang = pos[:, :, None] * freqs
    cos, sin = jnp.cos(ang), jnp.sin(ang)
    zero = jnp.zeros_like(sin)

    def table(first, second):
        tab = jnp.stack([first, second], axis=2)
        return jnp.tile(tab.reshape(seq_len, ATT_HEAD_DIM), (1, 2))

    return table(cos, cos), table(-sin, zero), table(zero, sin)


def _att_in_kernel(xp_ref, x_ref, xn_ref, sh_ref, sc_ref, g_ref, w_ref, cos_ref, sa_ref, sb_ref,
                   o_ref, h_scr, *, tm, tn, n_q_tiles, n_rope_tiles):
    del xp_ref, xn_ref
    j = pl.program_id(2)

    @pl.when(j == 0)
    def _():
        _norm_mod_rows(h_scr, 0, x_ref, g_ref, sc_ref, sh_ref)

    @pl.when(j < n_rope_tiles)
    def _():
        y = jnp.dot(h_scr[...], w_ref[...], preferred_element_type=F32)
        scale = jnp.where(j < n_q_tiles, QUERY_SCALE, 1.0)
        cos, sa, sb = cos_ref[...] * scale, sa_ref[...] * scale, sb_ref[...] * scale
        half = ATT_HEAD_DIM // 4
        for c in range(tn // LANES):
            yc = y[:, c * LANES:(c + 1) * LANES]
            r = (yc * cos + pltpu.roll(yc, LANES - half, axis=1) * sa
                 + pltpu.roll(yc, half, axis=1) * sb)
            o_ref[:, c * LANES:(c + 1) * LANES] = r.astype(o_ref.dtype)

    @pl.when(j >= n_rope_tiles)
    def _():
        o_ref[...] = jnp.dot(h_scr[...], w_ref[...], preferred_element_type=F32).astype(o_ref.dtype)


def _att_in(x, mods, norm_g, w_in, *, mod_row, tm, tn, col_lo, n_cols, rope):
    bsz, seq_len, _ = x.shape
    j0 = col_lo // tn
    if rope:
        tabs = _rope_tables(seq_len)
        n_q, n_rope = ATT_WIDTH // tn, 2 * ATT_WIDTH // tn
    else:
        tabs = (jnp.zeros((seq_len, LANES), F32),) * 3
        n_q = n_rope = 0
    kern = functools.partial(_att_in_kernel, tm=tm, tn=tn, n_q_tiles=n_q, n_rope_tiles=n_rope)
    tab_spec = pl.BlockSpec((tm, LANES), lambda b, i, j: (i, 0))
    return pl.pallas_call(
        kern,
        grid=(bsz, seq_len // tm, n_cols // tn),
        in_specs=_x_specs(tm, seq_len, mod_row) + [
            pl.BlockSpec((D_MODEL, tn), lambda b, i, j: (0, j0 + j)),
            tab_spec, tab_spec, tab_spec,
        ],
        out_specs=pl.BlockSpec((None, tm, tn), lambda b, i, j: (b, i, j)),
        out_shape=jax.ShapeDtypeStruct((bsz, seq_len, n_cols), BF16),
        scratch_shapes=[pltpu.VMEM((tm, D_MODEL), BF16)],
        compiler_params=_params("parallel", "parallel", "arbitrary"),
        name="att_in",
    )(x, x, x, mods, mods, norm_g, w_in, *tabs)


def _flash_kernel(lq1_ref, lk1_ref, lq2_ref, lk2_ref, sg_ref, q_ref, k_ref, v_ref, kc_ref, vc_ref,
                  z_ref, y_ref, m_scr, acc_scr, s_scr, p_scr, a_scr, sc_scr, pc_scr,
                  *, tq, tk, seq_len, lambda_init):
    q = q_ref[...]
    lane = lax.broadcasted_iota(jnp.int32, q.shape, 1)
    zero = jnp.zeros_like(q)
    qq = jnp.concatenate([jnp.where(lane < ATT_HEAD_DIM, q, zero),
                          jnp.where(lane >= ATT_HEAD_DIM, q, zero)], axis=0)

    m_scr[...] = jnp.full(m_scr.shape, -jnp.inf, F32)
    acc_scr[...] = jnp.zeros(acc_scr.shape, F32)
    n_chunks = seq_len // tk
    nt = (((1,), (1,)), ((), ()))

    def scores(kb):
        return lax.dot_general(qq, kb, nt, preferred_element_type=F32)

    def softmax(blocks, a_ref):
        for r in range(0, 2 * tq, FLASH_ROWS):
            rows = slice(r, r + FLASH_ROWS)
            cols = [s_ref[rows, j:j + LANES] for s_ref, _, width in blocks for j in range(0, width, LANES)]
            m_old = m_scr[rows, :]
            m_new = jnp.maximum(m_old, jnp.max(functools.reduce(jnp.maximum, cols), axis=1, keepdims=True))
            a_ref[rows, :] = jnp.exp2(m_old - m_new)
            m_scr[rows, :] = m_new
            for s_ref, p_ref, width in blocks:
                for j in range(0, width, LANES):
                    p_ref[rows, j:j + LANES] = jnp.exp2((s_ref[rows, j:j + LANES] - m_new).astype(BF16))

    def with_ones(vb):
        return jnp.concatenate([vb, jnp.ones_like(vb)], axis=1)

    def accumulate(a_ref, terms):
        pv = [jnp.dot(p_ref[...], with_ones(vb), preferred_element_type=F32) for p_ref, vb in terms]
        total, alpha = functools.reduce(jnp.add, pv), a_ref[...]
        for cs in (slice(0, HEAD_W), slice(HEAD_W, 2 * HEAD_W)):
            acc_scr[:, cs] = alpha * acc_scr[:, cs] + total[:, cs]

    def k_chunk(c):
        return k_ref[c * tk:(c + 1) * tk, :]

    def v_chunk(c):
        return v_ref[c * tk:(c + 1) * tk, :]

    ctx_w = kc_ref.shape[0]
    sc_scr[...] = scores(kc_ref[...])
    s_scr[0] = scores(k_chunk(0))
    softmax([(s_scr.at[0], p_scr.at[0], tk)], a_scr.at[0])
    s_scr[1] = scores(k_chunk(1))

    def stage(c, cur, prv, last):
        if not last:
            s_scr[prv] = scores(k_chunk(c + 1))
        accumulate(a_scr.at[prv], [(p_scr.at[prv], v_chunk(c - 1))])
        blocks = [(s_scr.at[cur], p_scr.at[cur], tk)]
        if last:
            blocks.append((sc_scr, pc_scr, ctx_w))
        softmax(blocks, a_scr.at[cur])

    for c in range(1, n_chunks - 1):
        stage(c, c % 2, (c + 1) % 2, False)
    stage(n_chunks - 1, 1, 0, True)
    accumulate(a_scr.at[1], [(p_scr.at[1], v_chunk(n_chunks - 1)), (pc_scr, vc_ref[...])])

    lam = (jnp.exp(jnp.sum(lq1_ref[...] * lk1_ref[...], axis=1, keepdims=True))
           - jnp.exp(jnp.sum(lq2_ref[...] * lk2_ref[...], axis=1, keepdims=True)) + lambda_init)
    o1 = acc_scr[0:tq, 0:HEAD_W] / acc_scr[0:tq, HEAD_W:]
    o2 = acc_scr[tq:, 0:HEAD_W] / acc_scr[tq:, HEAD_W:]
    o = o1 - lam * o2
    o = o * lax.rsqrt(jnp.mean(o * o, axis=-1, keepdims=True) + SUBLN_EPS)
    o = (o * sg_ref[...]) * (1.0 - lambda_init)
    y_ref[...] = (o * _silu(z_ref[...].astype(F32))).astype(y_ref.dtype)


def _flash(qkvz, kvc, lam_params, subln_g, *, lambda_init, tq, tk):
    bsz, seq_len, _ = qkvz.shape
    ctx_len = kvc.shape[1]
    nh = N_ATT_HEADS
    assert seq_len % (2 * tk) == 0 and ctx_len % LANES == 0
    kern = functools.partial(_flash_kernel, tq=tq, tk=tk, seq_len=seq_len, lambda_init=lambda_init)
    vec = pl.BlockSpec((1, ATT_HEAD_DIM), lambda b, h, i: (0, 0))
    return pl.pallas_call(
        kern,
        grid=(bsz, nh, seq_len // tq),
        in_specs=[vec, vec, vec, vec,
                  pl.BlockSpec((1, HEAD_W), lambda b, h, i: (0, 0)),
                  pl.BlockSpec((None, tq, HEAD_W), lambda b, h, i: (b, i, h)),
                  pl.BlockSpec((None, seq_len, HEAD_W), lambda b, h, i: (b, 0, nh + h)),
                  pl.BlockSpec((None, seq_len, HEAD_W), lambda b, h, i: (b, 0, 2 * nh + h)),
                  pl.BlockSpec((None, ctx_len, HEAD_W), lambda b, h, i: (b, 0, h)),
                  pl.BlockSpec((None, ctx_len, HEAD_W), lambda b, h, i: (b, 0, nh + h)),
                  pl.BlockSpec((None, tq, HEAD_W), lambda b, h, i: (b, i, 3 * nh + h))],
        out_specs=pl.BlockSpec((None, tq, HEAD_W), lambda b, h, i: (b, i, h)),
        out_shape=jax.ShapeDtypeStruct((bsz, seq_len, ATT_WIDTH), BF16),
        scratch_shapes=[pltpu.VMEM((2 * tq, LANES), F32), pltpu.VMEM((2 * tq, 2 * HEAD_W), F32),
                        pltpu.VMEM((2, 2 * tq, tk), F32), pltpu.VMEM((2, 2 * tq, tk), BF16),
                        pltpu.VMEM((2, 2 * tq, LANES), F32),
                        pltpu.VMEM((2 * tq, ctx_len), F32), pltpu.VMEM((2 * tq, ctx_len), BF16)],
        compiler_params=_params("parallel", "parallel", "arbitrary"),
        name="diff_flash",
    )(*lam_params, subln_g, qkvz, qkvz, qkvz, kvc, kvc, qkvz)


def _final_norm_kernel(x_ref, g_ref, o_ref):
    xv = x_ref[...]
    o_ref[...] = xv * lax.rsqrt(jnp.mean(xv * xv, axis=-1, keepdims=True) + EPS) * g_ref[...]


def _final_norm(x, g, *, tm):
    bsz, seq_len, _ = x.shape
    return pl.pallas_call(
        _final_norm_kernel,
        grid=(bsz, seq_len // tm),
        in_specs=[pl.BlockSpec((None, tm, D_MODEL), lambda b, i: (b, i, 0)),
                  pl.BlockSpec((1, D_MODEL), lambda b, i: (0, 0))],
        out_specs=pl.BlockSpec((None, tm, D_MODEL), lambda b, i: (b, i, 0)),
        out_shape=jax.ShapeDtypeStruct(x.shape, F32),
        compiler_params=_params("parallel", "parallel"),
        name="final_norm",
    )(x, g)


def kernel(x, c, ctx, c_ctx, norm_g, w_mod, b_mod, final_g, conv_w_in, conv_dw_w, conv_dw_b,
           conv_ln_g, conv_ln_b, conv_w_out, pool_w_in, pool_w_grp, pool_scale, pool_w_out,
           att_w_in, att_lam_q1, att_lam_k1, att_lam_q2, att_lam_k2, att_subln_g, att_w_out):
    bsz = x.shape[0]
    ctx_len = ctx.shape[1]
    n_att = DEPTH // N_MIXERS
    assert n_att <= 1
    last_ctx_reader = N_MIXERS * (n_att - 1) + ATT_ID if n_att > 0 else -1
    ctx_row = bsz

    cc = jnp.zeros((MOD_ROWS, D_MODEL), F32).at[:bsz].set(c).at[ctx_row].set(c_ctx)
    mods_all = _modulation(cc, w_mod, b_mod).reshape(DEPTH, MOD_ROWS, 3, 1, D_MODEL)

    tm, tn = TILES.tokens, TILES.cols
    fin_g = final_g.reshape(1, D_MODEL)
    final_fused = (DEPTH - 1) % N_MIXERS == CONV_ID
    xs, cs = x, ctx
    for i in range(DEPTH):
        kind, j = i % N_MIXERS, i // N_MIXERS
        ctx_update = i < last_ctx_reader
        mods = mods_all[i]
        g = norm_g[i].reshape(1, D_MODEL)
        last_layer = i == DEPTH - 1
        streams = [(xs, None, tm)]
        if ctx_update:
            streams.append((cs, ctx_row, ctx_len))
        new = []
        if kind == CONV_ID:
            w_in = conv_w_in[j].astype(BF16)
            w_out = conv_w_out[j].astype(BF16)
            for s, row, tms in streams:
                v, z = _conv_in(s, mods, g, w_in, conv_dw_w[j], conv_dw_b[j].reshape(1, E_WIDE),
                                mod_row=row, tm=tms, tn=tn)
                new.append(_conv_out(v, z, conv_ln_g[j].reshape(1, E_WIDE),
                                     conv_ln_b[j].reshape(1, E_WIDE), w_out, s, mods, fin_g,
                                     mod_row=row, tm=min(tms, TILES.conv_out_tokens),
                                     final_norm=last_layer and row is None))
        elif kind == POOL_ID:
            w_in = pool_w_in[j].astype(BF16)
            w_grp = pool_w_grp[j].astype(BF16)
            w_out = pool_w_out[j].astype(BF16)
            for s, row, tms in streams:
                act = _pool_in(s, mods, g, w_in, w_grp, pool_scale[j].reshape(1, E_WIDE),
                               mod_row=row, tm=tms)
                new.append(_proj_out(act, w_out, s, mods, mod_row=row, tm=tms))
        else:
            lambda_init = 0.8 - 0.6 * math.exp(-0.3 * i)
            w_in = att_w_in[j].astype(BF16)
            w_out = att_w_out[j].astype(BF16)
            qkvz = _att_in(xs, mods, g, w_in, mod_row=None, tm=tm, tn=ATT_WIDTH,
                           col_lo=0, n_cols=4 * ATT_WIDTH, rope=True)
            kvc = _att_in(cs, mods, g, w_in, mod_row=ctx_row, tm=ctx_len, tn=ATT_WIDTH,
                          col_lo=ATT_WIDTH, n_cols=2 * ATT_WIDTH, rope=False)
            lam_params = [p[j].reshape(1, ATT_HEAD_DIM)
                          for p in (att_lam_q1, att_lam_k1, att_lam_q2, att_lam_k2)]
            y = _flash(qkvz, kvc, lam_params, att_subln_g[j].reshape(1, HEAD_W),
                       lambda_init=lambda_init, tq=TILES.flash_q, tk=TILES.flash_k)
            new.append(_proj_out(y, w_out, xs, mods, mod_row=None, tm=tm))
        xs = new[0]
        if ctx_update:
            cs = new[1]
    return xs if final_fused else _final_norm(xs, fin_g, tm=tm)
```

```python
import functools
import math
from typing import NamedTuple

import jax
import jax.numpy as jnp
from jax import lax
from jax.experimental import pallas as pl
from jax.experimental.pallas import tpu as pltpu

D_MODEL = 2048
DEPTH = 4
GRID_W = 64
N_MIXERS = 3
CONV_ID, POOL_ID, ATT_ID = 0, 1, 2
E_WIDE = 2 * D_MODEL
CONV_K = 31
CONV_HALF = CONV_K // 2
POOL_WINDOWS = (2, 4, 8, 16)
POOL_GROUP = E_WIDE // len(POOL_WINDOWS)
ATT_WIDTH = D_MODEL
ATT_HEAD_DIM = 64
HEAD_W = 2 * ATT_HEAD_DIM
N_ATT_HEADS = ATT_WIDTH // HEAD_W
ROPE_BASE = 10000.0
EPS = 1e-6
SUBLN_EPS = 1e-5

LANES = 128
HALO = 16
MOD_ROWS = 8
VMEM_LIMIT = 56 * 1024 * 1024

F32 = jnp.float32
BF16 = jnp.bfloat16


class _Tiles(NamedTuple):
    tokens: int = 512
    cols: int = 512
    conv_out_tokens: int = 256
    flash_q: int = 512
    flash_k: int = 512
    mod_cols: int = 1024


TILES = _Tiles()


def _silu(v):
    return v * jax.nn.sigmoid(v)


def _params(*sem, flags=None):
    return pltpu.CompilerParams(dimension_semantics=sem, vmem_limit_bytes=VMEM_LIMIT, flags=flags)


NORM_ROWS = 32


def _norm_mod_rows(dst, r0, x_ref, g_ref, sc_ref, sh_ref):
    gain, sh = g_ref[...] * (1.0 + sc_ref[...]), sh_ref[...]
    n = x_ref.shape[0]
    for c in range(0, n, NORM_ROWS):
        m = min(NORM_ROWS, n - c)
        xv = x_ref[c:c + m, :].astype(F32)
        y = xv * lax.rsqrt(jnp.mean(xv * xv, axis=-1, keepdims=True) + EPS)
        dst[r0 + c:r0 + c + m, :] = (y * gain + sh).astype(BF16)


def _mod_kernel(c_ref, w_ref, b_ref, o_ref):
    s = _silu(c_ref[...])
    o_ref[...] = jnp.dot(s, w_ref[...], precision=lax.Precision.HIGHEST,
                         preferred_element_type=F32) + b_ref[...]


def _modulation(cc, w_mod, b_mod):
    tn = TILES.mod_cols
    n = 3 * D_MODEL
    return pl.pallas_call(
        _mod_kernel,
        grid=(DEPTH, n // tn),
        in_specs=[
            pl.BlockSpec((MOD_ROWS, D_MODEL), lambda l, j: (0, 0)),
            pl.BlockSpec((None, D_MODEL, tn), lambda l, j: (l, 0, j)),
            pl.BlockSpec((None, 1, tn), lambda l, j: (l, 0, j)),
        ],
        out_specs=pl.BlockSpec((None, MOD_ROWS, tn), lambda l, j: (l, 0, j)),
        out_shape=jax.ShapeDtypeStruct((DEPTH, MOD_ROWS, n), F32),
        compiler_params=_params("parallel", "parallel"),
        name="modulation",
    )(cc, w_mod, b_mod.reshape(DEPTH, 1, n))


def _x_specs(tm, seq_len, mod_row):
    nh = tm // HALO
    last = seq_len // HALO - 1
    row = (lambda b: b) if mod_row is None else (lambda b: mod_row)
    return [
        pl.BlockSpec((None, HALO, D_MODEL), lambda b, i, j: (b, jnp.maximum(i * nh - 1, 0), 0)),
        pl.BlockSpec((None, tm, D_MODEL), lambda b, i, j: (b, i, 0)),
        pl.BlockSpec((None, HALO, D_MODEL), lambda b, i, j: (b, jnp.minimum((i + 1) * nh, last), 0)),
        pl.BlockSpec((None, None, 1, D_MODEL), lambda b, i, j: (row(b), 0, 0, 0)),
        pl.BlockSpec((None, None, 1, D_MODEL), lambda b, i, j: (row(b), 1, 0, 0)),
        pl.BlockSpec((1, D_MODEL), lambda b, i, j: (0, 0)),
    ]


def _fill_h(h_scr, xp_ref, x_ref, xn_ref, sh_ref, sc_ref, g_ref, tm):
    _norm_mod_rows(h_scr, 0, xp_ref, g_ref, sc_ref, sh_ref)
    _norm_mod_rows(h_scr, HALO, x_ref, g_ref, sc_ref, sh_ref)
    _norm_mod_rows(h_scr, HALO + tm, xn_ref, g_ref, sc_ref, sh_ref)


def _seq_row_mask(i, tm, seq_len, tn):
    row = lax.broadcasted_iota(jnp.int32, (tm + 2 * HALO, tn), 0) + (i * tm - HALO)
    return (row >= 0) & (row < seq_len)


SUB = 8
CONV_GROUPS = 4
SHIFT_ROWS = 64


def _conv_in_kernel(xp_ref, x_ref, xn_ref, sh_ref, sc_ref, g_ref, wa_ref, wb_ref, wz_ref,
                    dw_ref, db_ref, v_ref, z_ref, h_scr, u_scr, us_scr, wt_scr, *, tm, tn, seq_len):
    i, j = pl.program_id(1), pl.program_id(2)
    ext = tm + 2 * HALO
    n_grp = ext // SUB
    hw = tn // 2

    @pl.when(j == 0)
    def _():
        _fill_h(h_scr, xp_ref, x_ref, xn_ref, sh_ref, sc_ref, g_ref, tm)

    for k in range(CONV_K):
        wt_scr[k] = jnp.broadcast_to(dw_ref[k:k + 1, :], (SUB, tn))
    bias = jnp.broadcast_to(db_ref[...], (SUB, tn))
    mask = _seq_row_mask(i, tm, seq_len, hw)
    h = h_scr[...]
    halves = (slice(0, hw), slice(hw, tn))

    def glu(cs):
        a = jnp.dot(h, wa_ref[:, cs], preferred_element_type=F32)
        b = jnp.dot(h, wb_ref[:, cs], preferred_element_type=F32)
        u = jnp.where(mask, a * jax.nn.sigmoid(b), 0.0)
        u_scr[:, cs] = u
        us_scr[0, :, :, cs] = u.reshape(n_grp, SUB, hw)

    def conv(cs):
        for r in range(1, SUB):
            for c0 in range(0, ext - SUB, SHIFT_ROWS):
                n = min(SHIFT_ROWS, ext - SUB - c0)
                us_scr[r, c0 // SUB:(c0 + n) // SUB, :, cs] = (
                    u_scr[pl.ds(c0 + r, n), cs].reshape(n // SUB, SUB, hw))
        for g0 in range(0, tm // SUB, CONV_GROUPS):
            acc = jnp.broadcast_to(bias[:, cs], (CONV_GROUPS, SUB, hw))
            for k in range(CONV_K):
                off = HALO - CONV_HALF + k
                lo = g0 + off // SUB
                acc = acc + us_scr[off % SUB, lo:lo + CONV_GROUPS, :, cs] * wt_scr[k, :, cs]
            v_ref[g0 * SUB:(g0 + CONV_GROUPS) * SUB, cs] = (
                acc.reshape(CONV_GROUPS * SUB, hw).astype(v_ref.dtype))

    glu(halves[0])
    glu(halves[1])
    conv(halves[0])
    z_ref[...] = jnp.dot(h_scr[HALO:HALO + tm, :], wz_ref[...],
                         preferred_element_type=F32).astype(z_ref.dtype)
    conv(halves[1])


def _conv_in(x, mods, norm_g, w_in, dw_w, dw_b, *, mod_row, tm, tn):
    bsz, seq_len, _ = x.shape
    nj = E_WIDE // tn
    kern = functools.partial(_conv_in_kernel, tm=tm, tn=tn, seq_len=seq_len)
    out = jax.ShapeDtypeStruct((bsz, seq_len, E_WIDE), BF16)
    return pl.pallas_call(
        kern,
        grid=(bsz, seq_len // tm, nj),
        in_specs=_x_specs(tm, seq_len, mod_row) + [
            pl.BlockSpec((D_MODEL, tn), lambda b, i, j: (0, j)),
            pl.BlockSpec((D_MODEL, tn), lambda b, i, j: (0, nj + j)),
            pl.BlockSpec((D_MODEL, tn), lambda b, i, j: (0, 2 * nj + j)),
            pl.BlockSpec((CONV_K, tn), lambda b, i, j: (0, j)),
            pl.BlockSpec((1, tn), lambda b, i, j: (0, j)),
        ],
        out_specs=[pl.BlockSpec((None, tm, tn), lambda b, i, j: (b, i, j))] * 2,
        out_shape=[out, out],
        scratch_shapes=[pltpu.VMEM((tm + 2 * HALO, D_MODEL), BF16),
                        pltpu.VMEM((tm + 2 * HALO, tn), F32),
                        pltpu.VMEM((SUB, (tm + 2 * HALO) // SUB, SUB, tn), F32),
                        pltpu.VMEM((CONV_K, SUB, tn), F32)],
        compiler_params=_params("parallel", "parallel", "arbitrary"),
        name="conv_in",
    )(x, x, x, mods, mods, norm_g, w_in, w_in, w_in, dw_w, dw_b)


LN_ROWS = 16
OUT_KC = 256


def _rmsnorm_rows(xv, g):
    return xv * lax.rsqrt(jnp.mean(xv * xv, axis=-1, keepdims=True) + EPS) * g


def _conv_out_kernel(v_ref, z_ref, lg_ref, lb_ref, w_ref, x_ref, gt_ref, fg_ref, o_ref, *, tm, final_norm):
    mus, rss = [], []
    for r in range(0, tm, LN_ROWS):
        v = v_ref[r:r + LN_ROWS, :].astype(F32)
        mu = jnp.mean(v, axis=-1, keepdims=True)
        d = v - mu
        mus.append(mu)
        rss.append(lax.rsqrt(jnp.mean(d * d, axis=-1, keepdims=True) + EPS))
    mu, rs = jnp.concatenate(mus, axis=0), jnp.concatenate(rss, axis=0)

    acc = None
    for k0 in range(0, E_WIDE, OUT_KC):
        ks = slice(k0, k0 + OUT_KC)
        y = _silu((v_ref[:, ks].astype(F32) - mu) * rs * lg_ref[:, ks] + lb_ref[:, ks])
        act = (y * _silu(z_ref[:, ks].astype(F32))).astype(BF16)
        part = jnp.dot(act, w_ref[ks, :], preferred_element_type=F32)
        acc = part if acc is None else acc + part
    o = x_ref[...] + gt_ref[...] * acc
    o_ref[...] = _rmsnorm_rows(o, fg_ref[...]) if final_norm else o


def _resident(shape):
    return pl.BlockSpec(shape, lambda *_: (0,) * len(shape), pipeline_mode=pl.Buffered(1))


def _conv_out(v, z, ln_g, ln_b, w_out, x, mods, final_g, *, mod_row, tm, final_norm):
    bsz, seq_len, _ = x.shape
    row = (lambda b: b) if mod_row is None else (lambda b: mod_row)
    kern = functools.partial(_conv_out_kernel, tm=tm, final_norm=final_norm)
    return pl.pallas_call(
        kern,
        grid=(bsz, seq_len // tm),
        in_specs=[
            pl.BlockSpec((None, tm, E_WIDE), lambda b, i: (b, i, 0)),
            pl.BlockSpec((None, tm, E_WIDE), lambda b, i: (b, i, 0)),
            _resident((1, E_WIDE)),
            _resident((1, E_WIDE)),
            _resident((E_WIDE, D_MODEL)),
            pl.BlockSpec((None, tm, D_MODEL), lambda b, i: (b, i, 0)),
            pl.BlockSpec((None, None, 1, D_MODEL), lambda b, i: (row(b), 2, 0, 0)),
            _resident((1, D_MODEL)),
        ],
        out_specs=pl.BlockSpec((None, tm, D_MODEL), lambda b, i: (b, i, 0)),
        out_shape=jax.ShapeDtypeStruct(x.shape, F32),
        compiler_params=_params("parallel", "parallel"),
        name="conv_out",
    )(v, z, ln_g, ln_b, w_out, x, mods, final_g)


POOL_ROWS = 64


def _pool_in_kernel(xp_ref, x_ref, xn_ref, sh_ref, sc_ref, g_ref, wu_ref, wz_ref, wg_ref,
                    ps_ref, o_ref, h_scr, u_scr, p_scr, *, tm, seq_len):
    i, grp = pl.program_id(1), pl.program_id(2)
    tn = POOL_GROUP

    @pl.when(grp == 0)
    def _():
        _fill_h(h_scr, xp_ref, x_ref, xn_ref, sh_ref, sc_ref, g_ref, tm)

    for gi, win in enumerate(POOL_WINDOWS):
        @pl.when(grp == gi)
        def _(win=win):
            half = win // 2
            u = jnp.dot(h_scr[...], wu_ref[...], preferred_element_type=F32)
            u_scr[...] = jnp.where(_seq_row_mask(i, tm, seq_len, tn), u, 0.0)
            z = jnp.dot(h_scr[HALO:HALO + tm, :], wz_ref[...], preferred_element_type=F32)

            for r0 in range(0, tm, POOL_ROWS):
                t = lax.broadcasted_iota(jnp.int32, (POOL_ROWS, 1), 0) + (i * tm + r0)
                cnt = jnp.minimum(t + half, seq_len) - jnp.maximum(t - half, 0)
                acc = u_scr[pl.ds(r0 + (HALO - half), POOL_ROWS), :]
                for d in range(1, win):
                    acc = acc + u_scr[pl.ds(r0 + (HALO - half + d), POOL_ROWS), :]
                p = acc / cnt.astype(F32) - u_scr[pl.ds(r0 + HALO, POOL_ROWS), :]
                p_scr[pl.ds(r0, POOL_ROWS), :] = p.astype(BF16)

            q = jnp.dot(p_scr[...], wg_ref[...], preferred_element_type=F32)
            o_ref[...] = (q * ps_ref[...] * _silu(z)).astype(o_ref.dtype)


def _pool_in(x, mods, norm_g, w_in, w_grp, scale, *, mod_row, tm):
    bsz, seq_len, _ = x.shape
    ng = len(POOL_WINDOWS)
    tn = POOL_GROUP
    kern = functools.partial(_pool_in_kernel, tm=tm, seq_len=seq_len)
    return pl.pallas_call(
        kern,
        grid=(bsz, seq_len // tm, ng),
        in_specs=_x_specs(tm, seq_len, mod_row) + [
            pl.BlockSpec((D_MODEL, tn), lambda b, i, j: (0, j)),
            pl.BlockSpec((D_MODEL, tn), lambda b, i, j: (0, ng + j)),
            pl.BlockSpec((None, tn, tn), lambda b, i, j: (j, 0, 0)),
            pl.BlockSpec((1, tn), lambda b, i, j: (0, j)),
        ],
        out_specs=pl.BlockSpec((None, tm, tn), lambda b, i, j: (b, i, j)),
        out_shape=jax.ShapeDtypeStruct((bsz, seq_len, E_WIDE), BF16),
        scratch_shapes=[pltpu.VMEM((tm + 2 * HALO, D_MODEL), BF16),
                        pltpu.VMEM((tm + 2 * HALO, tn), F32),
                        pltpu.VMEM((tm, tn), BF16)],
        compiler_params=_params("parallel", "parallel", "arbitrary"),
        name="pool_in",
    )(x, x, x, mods, mods, norm_g, w_in, w_in, w_grp, scale)


def _proj_out_kernel(a_ref, w_ref, x_ref, gt_ref, o_ref):
    y = jnp.dot(a_ref[...], w_ref[...], preferred_element_type=F32)
    o_ref[...] = x_ref[...] + gt_ref[...] * y


def _proj_out(act, w_out, x, mods, *, mod_row, tm):
    bsz, seq_len, _ = x.shape
    kdim = act.shape[-1]
    row = (lambda b: b) if mod_row is None else (lambda b: mod_row)
    return pl.pallas_call(
        _proj_out_kernel,
        grid=(bsz, seq_len // tm),
        in_specs=[
            pl.BlockSpec((None, tm, kdim), lambda b, i: (b, i, 0)),
            _resident((kdim, D_MODEL)),
            pl.BlockSpec((None, tm, D_MODEL), lambda b, i: (b, i, 0)),
            pl.BlockSpec((None, None, 1, D_MODEL), lambda b, i: (row(b), 2, 0, 0)),
        ],
        out_specs=pl.BlockSpec((None, tm, D_MODEL), lambda b, i: (b, i, 0)),
        out_shape=jax.ShapeDtypeStruct(x.shape, F32),
        compiler_params=_params("parallel", "parallel"),
        name="proj_out",
    )(act, w_out, x, mods)


QUERY_SCALE = ATT_HEAD_DIM ** -0.5 * math.log2(math.e)
FLASH_ROWS = 32


def _rope_tables(seq_len):
    t = jnp.arange(seq_len)
    n_freq = ATT_HEAD_DIM // 4
    freqs = ROPE_BASE ** (-jnp.arange(n_freq, dtype=F32) / n_freq)
    pos = jnp.stack([t // GRID_W, t % GRID_W], axis=1).astype(F32)
    ang = pos[:, :, None] * freqs
    cos, sin = jnp.cos(ang), jnp.sin(ang)
    zero = jnp.zeros_like(sin)

    def table(first, second):
        tab = jnp.stack([first, second], axis=2)
        return jnp.tile(tab.reshape(seq_len, ATT_HEAD_DIM), (1, 2))

    return table(cos, cos), table(-sin, zero), table(zero, sin)


def _att_in_kernel(xp_ref, x_ref, xn_ref, sh_ref, sc_ref, g_ref, w_ref, cos_ref, sa_ref, sb_ref,
                   o_ref, h_scr, *, tm, tn, n_q_tiles, n_rope_tiles):
    del xp_ref, xn_ref
    j = pl.program_id(2)

    @pl.when(j == 0)
    def _():
        _norm_mod_rows(h_scr, 0, x_ref, g_ref, sc_ref, sh_ref)

    @pl.when(j < n_rope_tiles)
    def _():
        y = jnp.dot(h_scr[...], w_ref[...], preferred_element_type=F32)
        scale = jnp.where(j < n_q_tiles, QUERY_SCALE, 1.0)
        cos, sa, sb = cos_ref[...] * scale, sa_ref[...] * scale, sb_ref[...] * scale
        half = ATT_HEAD_DIM // 4
        for c in range(tn // LANES):
            yc = y[:, c * LANES:(c + 1) * LANES]
            r = (yc * cos + pltpu.roll(yc, LANES - half, axis=1) * sa
                 + pltpu.roll(yc, half, axis=1) * sb)
            o_ref[:, c * LANES:(c + 1) * LANES] = r.astype(o_ref.dtype)

    @pl.when(j >= n_rope_tiles)
    def _():
        o_ref[...] = jnp.dot(h_scr[...], w_ref[...], preferred_element_type=F32).astype(o_ref.dtype)


def _att_in(x, mods, norm_g, w_in, *, mod_row, tm, tn, col_lo, n_cols, rope):
    bsz, seq_len, _ = x.shape
    j0 = col_lo // tn
    if rope:
        tabs = _rope_tables(seq_len)
        n_q, n_rope = ATT_WIDTH // tn, 2 * ATT_WIDTH // tn
    else:
        tabs = (jnp.zeros((seq_len, LANES), F32),) * 3
        n_q = n_rope = 0
    kern = functools.partial(_att_in_kernel, tm=tm, tn=tn, n_q_tiles=n_q, n_rope_tiles=n_rope)
    tab_spec = pl.BlockSpec((tm, LANES), lambda b, i, j: (i, 0))
    return pl.pallas_call(
        kern,
        grid=(bsz, seq_len // tm, n_cols // tn),
        in_specs=_x_specs(tm, seq_len, mod_row) + [
            pl.BlockSpec((D_MODEL, tn), lambda b, i, j: (0, j0 + j)),
            tab_spec, tab_spec, tab_spec,
        ],
        out_specs=pl.BlockSpec((None, tm, tn), lambda b, i, j: (b, i, j)),
        out_shape=jax.ShapeDtypeStruct((bsz, seq_len, n_cols), BF16),
        scratch_shapes=[pltpu.VMEM((tm, D_MODEL), BF16)],
        compiler_params=_params("parallel", "parallel", "arbitrary"),
        name="att_in",
    )(x, x, x, mods, mods, norm_g, w_in, *tabs)


def _flash_kernel(lq1_ref, lk1_ref, lq2_ref, lk2_ref, sg_ref, q_ref, k_ref, v_ref, kc_ref, vc_ref,
                  z_ref, y_ref, m_scr, acc_scr, s_scr, p_scr, a_scr, sc_scr, pc_scr,
                  *, tq, tk, seq_len, lambda_init):
    q = q_ref[...]
    lane = lax.broadcasted_iota(jnp.int32, q.shape, 1)
    zero = jnp.zeros_like(q)
    qq = jnp.concatenate([jnp.where(lane < ATT_HEAD_DIM, q, zero),
                          jnp.where(lane >= ATT_HEAD_DIM, q, zero)], axis=0)

    m_scr[...] = jnp.full(m_scr.shape, -jnp.inf, F32)
    acc_scr[...] = jnp.zeros(acc_scr.shape, F32)
    n_chunks = seq_len // tk
    nt = (((1,), (1,)), ((), ()))

    def scores(kb):
        return lax.dot_general(qq, kb, nt, preferred_element_type=F32)

    def softmax(blocks, a_ref):
        for r in range(0, 2 * tq, FLASH_ROWS):
            rows = slice(r, r + FLASH_ROWS)
            cols = [s_ref[rows, j:j + LANES] for s_ref, _, width in blocks for j in range(0, width, LANES)]
            m_old = m_scr[rows, :]
            m_new = jnp.maximum(m_old, jnp.max(functools.reduce(jnp.maximum, cols), axis=1, keepdims=True))
            a_ref[rows, :] = jnp.exp2(m_old - m_new)
            m_scr[rows, :] = m_new
            for s_ref, p_ref, width in blocks:
                for j in range(0, width, LANES):
                    p_ref[rows, j:j + LANES] = jnp.exp2((s_ref[rows, j:j + LANES] - m_new).astype(BF16))

    def with_ones(vb):
        return jnp.concatenate([vb, jnp.ones_like(vb)], axis=1)

    def accumulate(a_ref, terms):
        pv = [jnp.dot(p_ref[...], with_ones(vb), preferred_element_type=F32) for p_ref, vb in terms]
        total, alpha = functools.reduce(jnp.add, pv), a_ref[...]
        for cs in (slice(0, HEAD_W), slice(HEAD_W, 2 * HEAD_W)):
            acc_scr[:, cs] = alpha * acc_scr[:, cs] + total[:, cs]

    def k_chunk(c):
        return k_ref[c * tk:(c + 1) * tk, :]

    def v_chunk(c):
        return v_ref[c * tk:(c + 1) * tk, :]

    ctx_w = kc_ref.shape[0]
    sc_scr[...] = scores(kc_ref[...])
    s_scr[0] = scores(k_chunk(0))
    softmax([(s_scr.at[0], p_scr.at[0], tk)], a_scr.at[0])
    s_scr[1] = scores(k_chunk(1))

    def stage(c, cur, prv, last):
        if not last:
            s_scr[prv] = scores(k_chunk(c + 1))
        accumulate(a_scr.at[prv], [(p_scr.at[prv], v_chunk(c - 1))])
        blocks = [(s_scr.at[cur], p_scr.at[cur], tk)]
        if last:
            blocks.append((sc_scr, pc_scr, ctx_w))
        softmax(blocks, a_scr.at[cur])

    for c in range(1, n_chunks - 1):
        stage(c, c % 2, (c + 1) % 2, False)
    stage(n_chunks - 1, 1, 0, True)
    accumulate(a_scr.at[1], [(p_scr.at[1], v_chunk(n_chunks - 1)), (pc_scr, vc_ref[...])])

    lam = (jnp.exp(jnp.sum(lq1_ref[...] * lk1_ref[...], axis=1, keepdims=True))
           - jnp.exp(jnp.sum(lq2_ref[...] * lk2_ref[...], axis=1, keepdims=True)) + lambda_init)
    o1 = acc_scr[0:tq, 0:HEAD_W] / acc_scr[0:tq, HEAD_W:]
    o2 = acc_scr[tq:, 0:HEAD_W] / acc_scr[tq:, HEAD_W:]
    o = o1 - lam * o2
    o = o * lax.rsqrt(jnp.mean(o * o, axis=-1, keepdims=True) + SUBLN_EPS)
    o = (o * sg_ref[...]) * (1.0 - lambda_init)
    y_ref[...] = (o * _silu(z_ref[...].astype(F32))).astype(y_ref.dtype)


def _flash(qkvz, kvc, lam_params, subln_g, *, lambda_init, tq, tk):
    bsz, seq_len, _ = qkvz.shape
    ctx_len = kvc.shape[1]
    nh = N_ATT_HEADS
    assert seq_len % (2 * tk) == 0 and ctx_len % LANES == 0
    kern = functools.partial(_flash_kernel, tq=tq, tk=tk, seq_len=seq_len, lambda_init=lambda_init)
    vec = pl.BlockSpec((1, ATT_HEAD_DIM), lambda b, h, i: (0, 0))
    return pl.pallas_call(
        kern,
        grid=(bsz, nh, seq_len // tq),
        in_specs=[vec, vec, vec, vec,
                  pl.BlockSpec((1, HEAD_W), lambda b, h, i: (0, 0)),
                  pl.BlockSpec((None, tq, HEAD_W), lambda b, h, i: (b, i, h)),
                  pl.BlockSpec((None, seq_len, HEAD_W), lambda b, h, i: (b, 0, nh + h)),
                  pl.BlockSpec((None, seq_len, HEAD_W), lambda b, h, i: (b, 0, 2 * nh + h)),
                  pl.BlockSpec((None, ctx_len, HEAD_W), lambda b, h, i: (b, 0, h)),
                  pl.BlockSpec((None, ctx_len, HEAD_W), lambda b, h, i: (b, 0, nh + h)),
                  pl.BlockSpec((None, tq, HEAD_W), lambda b, h, i: (b, i, 3 * nh + h))],
        out_specs=pl.BlockSpec((None, tq, HEAD_W), lambda b, h, i: (b, i, h)),
        out_shape=jax.ShapeDtypeStruct((bsz, seq_len, ATT_WIDTH), BF16),
        scratch_shapes=[pltpu.VMEM((2 * tq, LANES), F32), pltpu.VMEM((2 * tq, 2 * HEAD_W), F32),
                        pltpu.VMEM((2, 2 * tq, tk), F32), pltpu.VMEM((2, 2 * tq, tk), BF16),
                        pltpu.VMEM((2, 2 * tq, LANES), F32),
                        pltpu.VMEM((2 * tq, ctx_len), F32), pltpu.VMEM((2 * tq, ctx_len), BF16)],
        compiler_params=_params("parallel", "parallel", "arbitrary"),
        name="diff_flash",
    )(*lam_params, subln_g, qkvz, qkvz, qkvz, kvc, kvc, qkvz)


def _final_norm_kernel(x_ref, g_ref, o_ref):
    xv = x_ref[...]
    o_ref[...] = xv * lax.rsqrt(jnp.mean(xv * xv, axis=-1, keepdims=True) + EPS) * g_ref[...]


def _final_norm(x, g, *, tm):
    bsz, seq_len, _ = x.shape
    return pl.pallas_call(
        _final_norm_kernel,
        grid=(bsz, seq_len // tm),
        in_specs=[pl.BlockSpec((None, tm, D_MODEL), lambda b, i: (b, i, 0)),
                  pl.BlockSpec((1, D_MODEL), lambda b, i: (0, 0))],
        out_specs=pl.BlockSpec((None, tm, D_MODEL), lambda b, i: (b, i, 0)),
        out_shape=jax.ShapeDtypeStruct(x.shape, F32),
        compiler_params=_params("parallel", "parallel"),
        name="final_norm",
    )(x, g)


def kernel(x, c, ctx, c_ctx, norm_g, w_mod, b_mod, final_g, conv_w_in, conv_dw_w, conv_dw_b,
           conv_ln_g, conv_ln_b, conv_w_out, pool_w_in, pool_w_grp, pool_scale, pool_w_out,
           att_w_in, att_lam_q1, att_lam_k1, att_lam_q2, att_lam_k2, att_subln_g, att_w_out):
    bsz = x.shape[0]
    ctx_len = ctx.shape[1]
    n_att = DEPTH // N_MIXERS
    assert n_att <= 1
    last_ctx_reader = N_MIXERS * (n_att - 1) + ATT_ID if n_att > 0 else -1
    ctx_row = bsz

    cc = jnp.zeros((MOD_ROWS, D_MODEL), F32).at[:bsz].set(c).at[ctx_row].set(c_ctx)
    mods_all = _modulation(cc, w_mod, b_mod).reshape(DEPTH, MOD_ROWS, 3, 1, D_MODEL)

    tm, tn = TILES.tokens, TILES.cols
    fin_g = final_g.reshape(1, D_MODEL)
    final_fused = (DEPTH - 1) % N_MIXERS == CONV_ID
    xs, cs = x, ctx
    for i in range(DEPTH):
        kind, j = i % N_MIXERS, i // N_MIXERS
        ctx_update = i < last_ctx_reader
        mods = mods_all[i]
        g = norm_g[i].reshape(1, D_MODEL)
        last_layer = i == DEPTH - 1
        streams = [(xs, None, tm)]
        if ctx_update:
            streams.append((cs, ctx_row, ctx_len))
        new = []
        if kind == CONV_ID:
            w_in = conv_w_in[j].astype(BF16)
            w_out = conv_w_out[j].astype(BF16)
            for s, row, tms in streams:
                v, z = _conv_in(s, mods, g, w_in, conv_dw_w[j], conv_dw_b[j].reshape(1, E_WIDE),
                                mod_row=row, tm=tms, tn=tn)
                new.append(_conv_out(v, z, conv_ln_g[j].reshape(1, E_WIDE),
                                     conv_ln_b[j].reshape(1, E_WIDE), w_out, s, mods, fin_g,
                                     mod_row=row, tm=min(tms, TILES.conv_out_tokens),
                                     final_norm=last_layer and row is None))
        elif kind == POOL_ID:
            w_in = pool_w_in[j].astype(BF16)
            w_grp = pool_w_grp[j].astype(BF16)
            w_out = pool_w_out[j].astype(BF16)
            for s, row, tms in streams:
                act = _pool_in(s, mods, g, w_in, w_grp, pool_scale[j].reshape(1, E_WIDE),
                               mod_row=row, tm=tms)
                new.append(_proj_out(act, w_out, s, mods, mod_row=row, tm=tms))
        else:
            lambda_init = 0.8 - 0.6 * math.exp(-0.3 * i)
            w_in = att_w_in[j].astype(BF16)
            w_out = att_w_out[j].astype(BF16)
            qkvz = _att_in(xs, mods, g, w_in, mod_row=None, tm=tm, tn=ATT_WIDTH,
                           col_lo=0, n_cols=4 * ATT_WIDTH, rope=True)
            kvc = _att_in(cs, mods, g, w_in, mod_row=ctx_row, tm=ctx_len, tn=ATT_WIDTH,
                          col_lo=ATT_WIDTH, n_cols=2 * ATT_WIDTH, rope=False)
            lam_params = [p[j].reshape(1, ATT_HEAD_DIM)
                          for p in (att_lam_q1, att_lam_k1, att_lam_q2, att_lam_k2)]
            y = _flash(qkvz, kvc, lam_params, att_subln_g[j].reshape(1, HEAD_W),
                       lambda_init=lambda_init, tq=TILES.flash_q, tk=TILES.flash_k)
            new.append(_proj_out(y, w_out, xs, mods, mod_row=None, tm=tm))
        xs = new[0]
        if ctx_update:
            cs = new[1]
    return xs if final_fused else _final_norm(xs, fin_g, tm=tm)
```
